```python
import math
import jax, jax.numpy as jnp
from jax import lax
import numpy as np

D_MODEL = 2048
BATCH = 1
SEQ = 16384
DEPTH = 4
DEC_BATCH = 8
DEC_SEQ = 16
PAST_LEN = 2048

CHUNK = 64
N_MIXERS = 2
N_HEADS = 16
Q_LORA = 512
KV_LORA = 512
QK_NOPE = 128
QK_ROPE = 64
V_HEAD = 128
ROPE_BASE = 10000.0
ATTN_SCALE = 1.0 / math.sqrt(QK_NOPE + QK_ROPE)
Q_BLOCK = 128
POOL_WINDOWS = (2, 4, 8, 16)
N_GROUPS = len(POOL_WINDOWS)
GROUP_W = D_MODEL // N_GROUPS
POOL_STATE = max(POOL_WINDOWS) - 1
D_FF = (D_MODEL * 7) // 2
N_EXPERTS = 8
TOP_K = 2
D_FF_E = (D_MODEL * 7) // 2
N_MLA = (DEPTH + 1) // 2
N_POOL = DEPTH // 2
N_DENSE = (DEPTH + 1) // 2
N_MOE = DEPTH // 2
ALPHA = (2.0 * DEPTH) ** 0.25
BETA = (8.0 * DEPTH) ** -0.25
LN_EPS = 1e-5
RMS_EPS = 1e-6

kernel_name = "hybrid_mla_pool_moe_stream_step"


def layer_norm(x, g, b):
    xf = x.astype(jnp.float32)
    mu = jnp.mean(xf, axis=-1, keepdims=True)
    var = jnp.mean(jnp.square(xf - mu), axis=-1, keepdims=True)
    return ((xf - mu) * lax.rsqrt(var + LN_EPS) * g.astype(jnp.float32) + b.astype(jnp.float32)).astype(x.dtype)


def rms_norm(x, g):
    xf = x.astype(jnp.float32)
    return (xf * lax.rsqrt(jnp.mean(xf * xf, axis=-1, keepdims=True) + RMS_EPS) * g.astype(jnp.float32)).astype(x.dtype)


def rope_tables(pos):
    inv = ROPE_BASE ** (-jnp.arange(0, QK_ROPE, 2, dtype=jnp.float32) / QK_ROPE)
    ang = pos.astype(jnp.float32)[:, None] * inv[None, :]
    return jnp.cos(ang), jnp.sin(ang)


def apply_rope(x, cos, sin):
    x1, x2 = jnp.split(x.astype(jnp.float32), 2, axis=-1)
    return jnp.concatenate([x1 * cos - x2 * sin, x2 * cos + x1 * sin], axis=-1).astype(x.dtype)


def mla_project(x, pos, w_dq, q_norm, w_uq, w_dkv, kv_norm):
    B, S, _ = x.shape
    cos, sin = rope_tables(pos)
    cq = rms_norm(x @ w_dq, q_norm)
    q = (cq @ w_uq).reshape(B, S, N_HEADS, QK_NOPE + QK_ROPE)
    q_nope = q[..., :QK_NOPE]
    q_rope = apply_rope(q[..., QK_NOPE:], cos[:, None, :], sin[:, None, :])
    kv_a = x @ w_dkv
    ckv = rms_norm(kv_a[..., :KV_LORA], kv_norm)
    k_rope = apply_rope(kv_a[..., KV_LORA:], cos, sin)
    return q_nope, q_rope, ckv, k_rope


def mla_expand(ckv, w_uk, w_uv):
    k_nope = jnp.einsum("btc,chd->bthd", ckv, w_uk)
    v = jnp.einsum("btc,chd->bthd", ckv, w_uv)
    return k_nope, v


def mla_attend(q_nope, q_rope, q_pos, k_nope, k_rope, v, k_pos):
    s = (jnp.einsum("bshd,bthd->bhst", q_nope, k_nope, preferred_element_type=jnp.float32)
         + jnp.einsum("bshr,btr->bhst", q_rope, k_rope, preferred_element_type=jnp.float32)) * ATTN_SCALE
    visible = (k_pos[None, :] // CHUNK) <= (q_pos[:, None] // CHUNK)
    s = jnp.where(visible[None, None], s, -jnp.inf)
    p = jax.nn.softmax(s, axis=-1).astype(v.dtype)
    return jnp.einsum("bhst,bthd->bshd", p, v)


def mla_prompt(x, pos, w_dq, q_norm, w_uq, w_dkv, kv_norm, w_uk, w_uv, w_o):
    B, S, _ = x.shape
    q_nope, q_rope, ckv, k_rope = mla_project(x, pos, w_dq, q_norm, w_uq, w_dkv, kv_norm)
    k_nope, v = mla_expand(ckv, w_uk, w_uv)
    nblk = S // Q_BLOCK

    def to_blocks(t):
        return jnp.moveaxis(t.reshape((B, nblk, Q_BLOCK) + t.shape[2:]), 1, 0)

    def one_block(args):
        qn, qr, qp = args
        return mla_attend(qn, qr, qp, k_nope, k_rope, v, pos)

    o = lax.map(one_block, (to_blocks(q_nope), to_blocks(q_rope), pos.reshape(nblk, Q_BLOCK)))
    o = jnp.moveaxis(o, 0, 1).reshape(B, S, N_HEADS * V_HEAD)
    return o @ w_o, ckv, k_rope


def mla_sample(x, pos, cache_ckv, cache_krope, w_dq, q_norm, w_uq, w_dkv, kv_norm, w_uk, w_uv, w_o):
    B, S, _ = x.shape
    q_nope, q_rope, ckv, k_rope = mla_project(x, pos, w_dq, q_norm, w_uq, w_dkv, kv_norm)
    ckv_all = jnp.concatenate([cache_ckv, ckv], axis=1)
    kr_all = jnp.concatenate([cache_krope, k_rope], axis=1)
    k_pos = jnp.arange(ckv_all.shape[1], dtype=jnp.int32)
    k_nope, v = mla_expand(ckv_all, w_uk, w_uv)
    o = mla_attend(q_nope, q_rope, pos, k_nope, kr_all, v, k_pos).reshape(B, S, N_HEADS * V_HEAD)
    return o @ w_o, ckv, k_rope


def pool_mix(x, hist, pos, w_pool, pool_scale):
    B, S, D = x.shape
    xp = jnp.concatenate([hist, x], axis=1).astype(jnp.float32)
    cs = jnp.concatenate([jnp.zeros((B, 1, D), jnp.float32), jnp.cumsum(xp, axis=1)], axis=1)
    end = cs[:, POOL_STATE + 1:, :]
    xf = x.astype(jnp.float32)
    diffs = []
    for g, w in enumerate(POOL_WINDOWS):
        lo, hi = g * GROUP_W, (g + 1) * GROUP_W
        start = cs[:, POOL_STATE + 1 - w:POOL_STATE + 1 - w + S, lo:hi]
        cnt = jnp.minimum(pos + 1, w).astype(jnp.float32)[None, :, None]
        diffs.append((end[..., lo:hi] - start) / cnt - xf[..., lo:hi])
    d = jnp.stack(diffs, axis=2).astype(x.dtype)
    y = jnp.einsum("bsgc,gcd->bsgd", d, w_pool).reshape(B, S, D)
    return y * pool_scale


def swiglu(x, w_gate, w_up, w_down):
    return (jax.nn.silu(x @ w_gate) * (x @ w_up)) @ w_down


def moe_swiglu(x, w_router, e_gate, e_up, e_down):
    logits = jnp.einsum("bsd,de->bse", x, w_router, preferred_element_type=jnp.float32)
    top_v, top_i = lax.top_k(logits, TOP_K)
    gates = jax.nn.softmax(top_v, axis=-1)
    dense_gate = jnp.sum(jax.nn.one_hot(top_i, N_EXPERTS, dtype=jnp.float32) * gates[..., None], axis=-2)
    y = jnp.zeros_like(x)
    for e in range(N_EXPERTS):
        y = y + swiglu(x, e_gate[e], e_up[e], e_down[e]) * dense_gate[..., e:e + 1].astype(x.dtype)
    return y


def setup_inputs(seed: int = 0) -> dict:
    key = jax.random.key(seed)
    ks = jax.random.split(key, 32)
    f32 = jnp.float32

    def nrm(k, shape, scale):
        return jax.random.normal(k, shape, f32) * scale

    H = N_HEADS
    return {
        "x_prompt": nrm(ks[0], (BATCH, SEQ, D_MODEL), 1.0),
        "x_sample": nrm(ks[1], (DEC_BATCH, DEC_SEQ, D_MODEL), 1.0),
        "cache_ckv": nrm(ks[2], (N_MLA, DEC_BATCH, PAST_LEN, KV_LORA), 1.0),
        "cache_krope": nrm(ks[3], (N_MLA, DEC_BATCH, PAST_LEN, QK_ROPE), 1.0),
        "state_pool": nrm(ks[4], (N_POOL, DEC_BATCH, POOL_STATE, D_MODEL), 1.0),
        "mla_w_dq": nrm(ks[5], (N_MLA, D_MODEL, Q_LORA), D_MODEL ** -0.5),
        "mla_q_norm": 1.0 + nrm(ks[6], (N_MLA, Q_LORA), 0.02),
        "mla_w_uq": nrm(ks[7], (N_MLA, Q_LORA, H * (QK_NOPE + QK_ROPE)), Q_LORA ** -0.5),
        "mla_w_dkv": nrm(ks[8], (N_MLA, D_MODEL, KV_LORA + QK_ROPE), D_MODEL ** -0.5),
        "mla_kv_norm": 1.0 + nrm(ks[9], (N_MLA, KV_LORA), 0.02),
        "mla_w_uk": nrm(ks[10], (N_MLA, KV_LORA, H, QK_NOPE), KV_LORA ** -0.5),
        "mla_w_uv": nrm(ks[11], (N_MLA, KV_LORA, H, V_HEAD), KV_LORA ** -0.5),
        "mla_w_o": nrm(ks[12], (N_MLA, H * V_HEAD, D_MODEL), BETA * (H * V_HEAD) ** -0.5),
        "pool_w": nrm(ks[13], (N_POOL, N_GROUPS, GROUP_W, GROUP_W), BETA * GROUP_W ** -0.5),
        "pool_scale": 1.0 + nrm(ks[14], (N_POOL, D_MODEL), 0.1),
        "ffn_w_gate": nrm(ks[15], (N_DENSE, D_MODEL, D_FF), D_MODEL ** -0.5),
        "ffn_w_up": nrm(ks[16], (N_DENSE, D_MODEL, D_FF), D_MODEL ** -0.5),
        "ffn_w_down": nrm(ks[17], (N_DENSE, D_FF, D_MODEL), BETA * D_FF ** -0.5),
        "moe_w_router": nrm(ks[18], (N_MOE, D_MODEL, N_EXPERTS), D_MODEL ** -0.5),
        "moe_w_gate": nrm(ks[19], (N_MOE, N_EXPERTS, D_MODEL, D_FF_E), D_MODEL ** -0.5),
        "moe_w_up": nrm(ks[20], (N_MOE, N_EXPERTS, D_MODEL, D_FF_E), D_MODEL ** -0.5),
        "moe_w_down": nrm(ks[21], (N_MOE, N_EXPERTS, D_FF_E, D_MODEL), BETA * D_FF_E ** -0.5),
        "ln_mix_g": 1.0 + nrm(ks[22], (DEPTH, D_MODEL), 0.02),
        "ln_mix_b": nrm(ks[23], (DEPTH, D_MODEL), 0.02),
        "ln_ffn_g": 1.0 + nrm(ks[24], (DEPTH, D_MODEL), 0.02),
        "ln_ffn_b": nrm(ks[25], (DEPTH, D_MODEL), 0.02),
    }


def reference(x_prompt, x_sample, cache_ckv, cache_krope, state_pool,
              mla_w_dq, mla_q_norm, mla_w_uq, mla_w_dkv, mla_kv_norm, mla_w_uk, mla_w_uv, mla_w_o,
              pool_w, pool_scale,
              ffn_w_gate, ffn_w_up, ffn_w_down,
              moe_w_router, moe_w_gate, moe_w_up, moe_w_down,
              ln_mix_g, ln_mix_b, ln_ffn_g, ln_ffn_b):
    n_p = x_prompt.shape[1]
    n_s = x_sample.shape[1]
    past = cache_ckv.shape[2]
    pos_p = jnp.arange(n_p, dtype=jnp.int32)
    pos_s = past + jnp.arange(n_s, dtype=jnp.int32)
    hist_p = jnp.zeros((x_prompt.shape[0], POOL_STATE, D_MODEL), x_prompt.dtype)

    xp, xs = x_prompt, x_sample
    ckv_p, kr_p, pool_p = [], [], []
    ckv_s, kr_s, pool_s = [], [], []
    for i in range(DEPTH):
        j = i // N_MIXERS
        if i % N_MIXERS == 0:
            mla_w = (mla_w_dq[j], mla_q_norm[j], mla_w_uq[j], mla_w_dkv[j], mla_kv_norm[j],
                     mla_w_uk[j], mla_w_uv[j], mla_w_o[j])
            yp, c_new_p, r_new_p = mla_prompt(xp, pos_p, *mla_w)
            ys, c_new_s, r_new_s = mla_sample(xs, pos_s, cache_ckv[j], cache_krope[j], *mla_w)
            ckv_p.append(c_new_p)
            kr_p.append(r_new_p)
            ckv_s.append(c_new_s)
            kr_s.append(r_new_s)
        else:
            yp = pool_mix(xp, hist_p, pos_p, pool_w[j], pool_scale[j])
            ys = pool_mix(xs, state_pool[j], pos_s, pool_w[j], pool_scale[j])
            pool_p.append(xp[:, -POOL_STATE:, :])
            pool_s.append(jnp.concatenate([state_pool[j], xs], axis=1)[:, -POOL_STATE:, :])
        xp = layer_norm(ALPHA * xp + yp, ln_mix_g[i], ln_mix_b[i])
        xs = layer_norm(ALPHA * xs + ys, ln_mix_g[i], ln_mix_b[i])

        k = i // 2
        if i % 2 == 0:
            fp = swiglu(xp, ffn_w_gate[k], ffn_w_up[k], ffn_w_down[k])
            fs = swiglu(xs, ffn_w_gate[k], ffn_w_up[k], ffn_w_down[k])
        else:
            fp = moe_swiglu(xp, moe_w_router[k], moe_w_gate[k], moe_w_up[k], moe_w_down[k])
            fs = moe_swiglu(xs, moe_w_router[k], moe_w_gate[k], moe_w_up[k], moe_w_down[k])
        xp = layer_norm(ALPHA * xp + fp, ln_ffn_g[i], ln_ffn_b[i])
        xs = layer_norm(ALPHA * xs + fs, ln_ffn_g[i], ln_ffn_b[i])

    new_ckv_prompt = jnp.stack(ckv_p)
    new_krope_prompt = jnp.stack(kr_p)
    new_pool_prompt = jnp.stack(pool_p)
    new_ckv_sample = jnp.stack(ckv_s)
    new_krope_sample = jnp.stack(kr_s)
    new_pool_sample = jnp.stack(pool_s)
    return (xp, xs, new_ckv_prompt, new_krope_prompt, new_pool_prompt,
            new_ckv_sample, new_krope_sample, new_pool_sample)
```

```python
import functools
import math

import jax
import jax.numpy as jnp
from jax import lax
from jax.experimental import pallas as pl
from jax.experimental.pallas import tpu as pltpu

BF16 = jnp.bfloat16
F32 = jnp.float32

CHUNK = 64
POOL_WINDOWS = (2, 4, 8, 16)
POOL_HALO = 16
ROPE_BASE = 10000.0
LN_EPS = 1e-5
RMS_EPS = 1e-6
TOP_K = 2

V7X_LANES = 128
V7X_VMEM_BUDGET = 56 * 1024 * 1024


def _cparams(*sem):
    return pltpu.CompilerParams(dimension_semantics=sem, vmem_limit_bytes=V7X_VMEM_BUDGET)


def _tile(n, pref):
    if n <= pref:
        return n
    t = pref
    while n % t:
        t //= 2
    assert t >= 8, (n, pref)
    return t


def _layer_norm(z, g, b):
    mu = jnp.mean(z, axis=-1, keepdims=True)
    zc = z - mu
    var = jnp.mean(zc * zc, axis=-1, keepdims=True)
    return zc * lax.rsqrt(var + LN_EPS) * g + b


def _rms(c, g):
    return c * lax.rsqrt(jnp.mean(c * c, axis=-1, keepdims=True) + RMS_EPS) * g


def _mla_proj_kernel(x_ref, w1_ref, w2_ref, qn_ref, kvn_ref, t1_ref, t2_ref,
                     q_ref, ckv_ref, kr_ref, ckr_ref, *, ql, kl, rope, heads, hd):
    xb = x_ref[...].astype(BF16)
    r = jnp.dot(xb, w1_ref[...], preferred_element_type=F32)
    cq = _rms(r[:, :ql], qn_ref[...])
    ckv = _rms(r[:, ql:ql + kl], kvn_ref[...])
    t = r[:, ql + kl:] * t1_ref[...]
    kk = t + pltpu.roll(t, rope, axis=1)
    ckv_ref[...] = ckv
    kr_ref[...] = kk[:, :rope]
    ckr_ref[:, :kl] = ckv.astype(BF16)
    ckr_ref[:, kl:] = kk.astype(BF16)
    cqb = cq.astype(BF16)
    t2 = t2_ref[...]
    for h in range(heads):
        qh = jnp.dot(cqb, w2_ref[:, h * hd:(h + 1) * hd], preferred_element_type=F32)
        q_ref[h] = (qh * t2).astype(BF16)


def _mla_proj(x, w1, w2, qn, kvn, t1, t2, *, ql, kl, rope, heads, hd):
    m, d = x.shape
    bm = _tile(m, 256)
    kern = functools.partial(_mla_proj_kernel, ql=ql, kl=kl, rope=rope, heads=heads, hd=hd)
    row = lambda i: (i, 0)
    full = lambda i: (0, 0)
    return pl.pallas_call(
        kern,
        grid=(m // bm,),
        in_specs=[
            pl.BlockSpec((bm, d), row),
            pl.BlockSpec(w1.shape, full),
            pl.BlockSpec(w2.shape, full),
            pl.BlockSpec((1, ql), full),
            pl.BlockSpec((1, kl), full),
            pl.BlockSpec((bm, 2 * rope), row),
            pl.BlockSpec((bm, hd), row),
        ],
        out_specs=[
            pl.BlockSpec((heads, bm, hd), lambda i: (0, i, 0)),
            pl.BlockSpec((bm, kl), row),
            pl.BlockSpec((bm, rope), row),
            pl.BlockSpec((bm, kl + 2 * rope), row),
        ],
        out_shape=[
            jax.ShapeDtypeStruct((heads, m, hd), BF16),
            jax.ShapeDtypeStruct((m, kl), F32),
            jax.ShapeDtypeStruct((m, rope), F32),
            jax.ShapeDtypeStruct((m, kl + 2 * rope), BF16),
        ],
        compiler_params=_cparams("parallel"),
        name="mla_proj",
    )(x, w1, w2, qn, kvn, t1, t2)


def _kv_up_kernel(ckr_ref, wk_ref, wv_ref, k_ref, v_ref, *, kl, heads, nope, vh):
    c = ckr_ref[:, :kl]
    kk = ckr_ref[:, kl:]
    for p in range(heads // 2):
        kn = jnp.dot(c, wk_ref[:, 2 * p * nope:(2 * p + 2) * nope], preferred_element_type=F32)
        vv = jnp.dot(c, wv_ref[:, 2 * p * vh:(2 * p + 2) * vh], preferred_element_type=F32)
        for s in range(2):
            h = 2 * p + s
            k_ref[h, :, :nope] = kn[:, s * nope:(s + 1) * nope].astype(BF16)
            k_ref[h, :, nope:] = kk
            v_ref[h] = vv[:, s * vh:(s + 1) * vh].astype(BF16)


def _kv_up(ckr, wk, wv, *, kl, heads, nope, vh, hd):
    m = ckr.shape[0]
    bm = _tile(m, 512)
    kern = functools.partial(_kv_up_kernel, kl=kl, heads=heads, nope=nope, vh=vh)
    return pl.pallas_call(
        kern,
        grid=(m // bm,),
        in_specs=[
            pl.BlockSpec((bm, ckr.shape[1]), lambda i: (i, 0)),
            pl.BlockSpec(wk.shape, lambda i: (0, 0)),
            pl.BlockSpec(wv.shape, lambda i: (0, 0)),
        ],
        out_specs=[
            pl.BlockSpec((heads, bm, hd), lambda i: (0, i, 0)),
            pl.BlockSpec((heads, bm, vh), lambda i: (0, i, 0)),
        ],
        out_shape=[
            jax.ShapeDtypeStruct((heads, m, hd), BF16),
            jax.ShapeDtypeStruct((heads, m, vh), BF16),
        ],
        compiler_params=_cparams("parallel"),
        name="kv_up",
    )(ckr, wk, wv)


def _attn_prompt_kernel(q_ref, k_ref, v_ref, o_ref, m_scr, l_scr, acc_scr, *, blk):
    i = pl.program_id(1)
    q = q_ref[...]
    m_scr[...] = jnp.full(m_scr.shape, -jnp.inf, F32)
    l_scr[...] = jnp.zeros(l_scr.shape, F32)
    acc_scr[...] = jnp.zeros(acc_scr.shape, F32)

    def block(j, masked):
        start = pl.multiple_of(j * blk, blk)
        k = k_ref[pl.ds(start, blk), :]
        v = v_ref[pl.ds(start, blk), :]
        s = lax.dot_general(q, k, (((1,), (1,)), ((), ())), preferred_element_type=F32)
        if masked:
            qc = lax.broadcasted_iota(jnp.int32, s.shape, 0) // CHUNK
            kc = lax.broadcasted_iota(jnp.int32, s.shape, 1) // CHUNK
            s = jnp.where(kc <= qc, s, -jnp.inf)
        m_prev = m_scr[...]
        m_new = jnp.maximum(m_prev, jnp.max(s, axis=-1, keepdims=True))
        alpha = jnp.exp(m_prev - m_new)
        p = jnp.exp(s - m_new)
        l_scr[...] = alpha * l_scr[...] + jnp.sum(p, axis=-1, keepdims=True)
        acc_scr[...] = alpha * acc_scr[...] + jnp.dot(p.astype(BF16), v, preferred_element_type=F32)
        m_scr[...] = m_new

    def body(j, carry):
        block(j, False)
        return carry

    lax.fori_loop(0, i, body, 0)
    block(i, True)
    o_ref[...] = (acc_scr[...] / l_scr[...]).astype(o_ref.dtype)


def _attn_prompt(q, k, v, *, blk):
    heads, s, hd = q.shape
    vh = v.shape[2]
    kern = functools.partial(_attn_prompt_kernel, blk=blk)
    return pl.pallas_call(
        kern,
        grid=(heads, s // blk),
        in_specs=[
            pl.BlockSpec((None, blk, hd), lambda h, i: (h, i, 0)),
            pl.BlockSpec((None, s, hd), lambda h, i: (h, 0, 0)),
            pl.BlockSpec((None, s, vh), lambda h, i: (h, 0, 0)),
        ],
        out_specs=pl.BlockSpec((blk, vh), lambda h, i: (i, h)),
        out_shape=jax.ShapeDtypeStruct((s, heads * vh), BF16),
        scratch_shapes=[
            pltpu.VMEM((blk, 1), F32),
            pltpu.VMEM((blk, 1), F32),
            pltpu.VMEM((blk, vh), F32),
        ],
        compiler_params=_cparams("parallel", "arbitrary"),
        name="attn_prompt",
    )(q, k, v)


def _attn_sample_kernel(q_ref, kc_ref, vc_ref, kn_ref, vn_ref, o_ref, *, past):
    q = q_ref[...]
    dn = (((1,), (1,)), ((), ()))
    s1 = lax.dot_general(q, kc_ref[...], dn, preferred_element_type=F32)
    s2 = lax.dot_general(q, kn_ref[...], dn, preferred_element_type=F32)
    qc1 = (past + lax.broadcasted_iota(jnp.int32, s1.shape, 0)) // CHUNK
    kc1 = lax.broadcasted_iota(jnp.int32, s1.shape, 1) // CHUNK
    s1 = jnp.where(kc1 <= qc1, s1, -jnp.inf)
    qc2 = (past + lax.broadcasted_iota(jnp.int32, s2.shape, 0)) // CHUNK
    kc2 = (past + lax.broadcasted_iota(jnp.int32, s2.shape, 1)) // CHUNK
    s2 = jnp.where(kc2 <= qc2, s2, -jnp.inf)
    m = jnp.maximum(jnp.max(s1, axis=-1, keepdims=True), jnp.max(s2, axis=-1, keepdims=True))
    p1 = jnp.exp(s1 - m)
    p2 = jnp.exp(s2 - m)
    l = jnp.sum(p1, axis=-1, keepdims=True) + jnp.sum(p2, axis=-1, keepdims=True)
    o = (jnp.dot(p1.astype(BF16), vc_ref[...], preferred_element_type=F32)
         + jnp.dot(p2.astype(BF16), vn_ref[...], preferred_element_type=F32))
    o_ref[...] = (o / l).astype(o_ref.dtype)


def _attn_sample(q, kc, vc, kn, vn, *, batch, past):
    heads, m, hd = q.shape
    sq = m // batch
    vh = vc.shape[2]
    kern = functools.partial(_attn_sample_kernel, past=past)
    return pl.pallas_call(
        kern,
        grid=(batch, heads),
        in_specs=[
            pl.BlockSpec((None, sq, hd), lambda b, h: (h, b, 0)),
            pl.BlockSpec((None, past, hd), lambda b, h: (h, b, 0)),
            pl.BlockSpec((None, past, vh), lambda b, h: (h, b, 0)),
            pl.BlockSpec((None, sq, hd), lambda b, h: (h, b, 0)),
            pl.BlockSpec((None, sq, vh), lambda b, h: (h, b, 0)),
        ],
        out_specs=pl.BlockSpec((sq, vh), lambda b, h: (b, h)),
        out_shape=jax.ShapeDtypeStruct((m, heads * vh), BF16),
        compiler_params=_cparams("parallel", "parallel"),
        name="attn_sample",
    )(q, kc, vc, kn, vn)


def _proj_res_ln_kernel(a_ref, w_ref, x_ref, g_ref, b_ref, o_ref, *, alpha):
    y = jnp.dot(a_ref[...], w_ref[...], preferred_element_type=F32)
    o_ref[...] = _layer_norm(alpha * x_ref[...] + y, g_ref[...], b_ref[...])


def _proj_res_ln(a, w, x, g, b, *, alpha):
    m, k = a.shape
    d = w.shape[1]
    bm = _tile(m, 512)
    row = lambda i: (i, 0)
    full = lambda i: (0, 0)
    return pl.pallas_call(
        functools.partial(_proj_res_ln_kernel, alpha=alpha),
        grid=(m // bm,),
        in_specs=[
            pl.BlockSpec((bm, k), row),
            pl.BlockSpec((k, d), full),
            pl.BlockSpec((bm, d), row),
            pl.BlockSpec((1, d), full),
            pl.BlockSpec((1, d), full),
        ],
        out_specs=pl.BlockSpec((bm, d), row),
        out_shape=jax.ShapeDtypeStruct((m, d), F32),
        compiler_params=_cparams("parallel"),
        name="proj_res_ln",
    )(a, w, x, g, b)


def _pool_kernel(x_ref, prev_ref, hist_ref, w_ref, sc_ref, g_ref, b_ref, o_ref, buf, y_scr,
                 *, bm, gw, pos0, alpha):
    i = pl.program_id(1)
    x = x_ref[...]
    halo = jnp.where(i == 0, hist_ref[...], prev_ref[...])
    buf[0:POOL_HALO, :] = halo
    buf[POOL_HALO:POOL_HALO + bm, :] = x
    pos = pos0 + i * bm + lax.broadcasted_iota(jnp.int32, (bm, 1), 0)
    for gi, w in enumerate(POOL_WINDOWS):
        lo, hi = gi * gw, (gi + 1) * gw
        tot = x[:, lo:hi]
        for j in range(1, w):
            tot = tot + buf[POOL_HALO - j:POOL_HALO - j + bm, lo:hi]
        cnt = jnp.minimum(pos + 1, w).astype(F32)
        dlt = (tot / cnt - x[:, lo:hi]).astype(BF16)
        y_scr[:, lo:hi] = jnp.dot(dlt, w_ref[gi], preferred_element_type=F32)
    y = y_scr[...] * sc_ref[...]
    o_ref[...] = _layer_norm(alpha * x + y, g_ref[...], b_ref[...])


def _pool_layer(x, hist, w, sc, g, b, *, pos0, alpha):
    bsz, s, d = x.shape
    gw = d // len(POOL_WINDOWS)
    bm = _tile(s, 512)
    assert bm % POOL_HALO == 0
    hb = bm // POOL_HALO
    kern = functools.partial(_pool_kernel, bm=bm, gw=gw, pos0=pos0, alpha=alpha)
    vec = lambda bb, i: (0, 0)
    return pl.pallas_call(
        kern,
        grid=(bsz, s // bm),
        in_specs=[
            pl.BlockSpec((None, bm, d), lambda bb, i: (bb, i, 0)),
            pl.BlockSpec((None, POOL_HALO, d), lambda bb, i: (bb, jnp.maximum(i * hb - 1, 0), 0)),
            pl.BlockSpec((None, POOL_HALO, d), lambda bb, i: (bb, 0, 0)),
            pl.BlockSpec(w.shape, lambda bb, i: (0, 0, 0)),
            pl.BlockSpec((1, d), vec),
            pl.BlockSpec((1, d), vec),
            pl.BlockSpec((1, d), vec),
        ],
        out_specs=pl.BlockSpec((None, bm, d), lambda bb, i: (bb, i, 0)),
        out_shape=jax.ShapeDtypeStruct((bsz, s, d), F32),
        scratch_shapes=[
            pltpu.VMEM((bm + POOL_HALO, d), F32),
            pltpu.VMEM((bm, d), F32),
        ],
        compiler_params=_cparams("parallel", "arbitrary"),
        name="pool_mix",
    )(x, x, hist, w, sc, g, b)


def _swiglu_tile(tv_ref, x_ref, wg_ref, wu_ref, wd_ref, xb_scr, acc_scr, *, bm, sb):
    i = pl.program_id(0)
    f = pl.program_id(1)
    valid = tv_ref[i]

    @pl.when(f == 0)
    def _():
        acc_scr[...] = jnp.zeros(acc_scr.shape, F32)
        xb_scr[...] = x_ref[...].astype(BF16)

    for s in range(bm // sb):
        @pl.when(s * sb < valid)
        def _():
            xs = xb_scr[s * sb:(s + 1) * sb, :]
            gte = jnp.dot(xs, wg_ref[...], preferred_element_type=F32)
            up = jnp.dot(xs, wu_ref[...], preferred_element_type=F32)
            hid = (gte * jax.nn.sigmoid(gte) * up).astype(BF16)
            acc_scr[s * sb:(s + 1) * sb, :] += jnp.dot(hid, wd_ref[...], preferred_element_type=F32)


def _ffn_dense_kernel(te_ref, tv_ref, x_ref, wg_ref, wu_ref, wd_ref, g_ref, b_ref, o_ref,
                      xb_scr, acc_scr, *, bm, sb, alpha):
    del te_ref
    _swiglu_tile(tv_ref, x_ref, wg_ref, wu_ref, wd_ref, xb_scr, acc_scr, bm=bm, sb=sb)

    @pl.when(pl.program_id(1) == pl.num_programs(1) - 1)
    def _():
        o_ref[...] = _layer_norm(alpha * x_ref[...] + acc_scr[...], g_ref[...], b_ref[...])


def _ffn_expert_kernel(te_ref, tv_ref, x_ref, wg_ref, wu_ref, wd_ref, o_ref,
                       xb_scr, acc_scr, *, bm, sb):
    del te_ref
    _swiglu_tile(tv_ref, x_ref, wg_ref, wu_ref, wd_ref, xb_scr, acc_scr, bm=bm, sb=sb)

    @pl.when(pl.program_id(1) == pl.num_programs(1) - 1)
    def _():
        o_ref[...] = acc_scr[...]


def _ffn_call(x, wg, wu, wd, tile_expert, tile_valid, *, bm, ln=None, alpha=None):
    r, d = x.shape
    nf_total = wg.shape[2]
    bf = _tile(nf_total, 512)
    nf = nf_total // bf
    sb = _tile(bm, 256)
    nt = r // bm

    def chunk(i, f, te, tv):
        return jnp.where(tv[i] > 0, f, nf - 1)

    in_specs = [
        pl.BlockSpec((bm, d), lambda i, f, te, tv: (i, 0)),
        pl.BlockSpec((None, d, bf), lambda i, f, te, tv: (te[i], 0, chunk(i, f, te, tv))),
        pl.BlockSpec((None, d, bf), lambda i, f, te, tv: (te[i], 0, chunk(i, f, te, tv))),
        pl.BlockSpec((None, bf, d), lambda i, f, te, tv: (te[i], chunk(i, f, te, tv), 0)),
    ]
    args = [x, wg, wu, wd]
    if ln is not None:
        in_specs += [pl.BlockSpec((1, d), lambda i, f, te, tv: (0, 0))] * 2
        args += list(ln)
        kern = functools.partial(_ffn_dense_kernel, bm=bm, sb=sb, alpha=alpha)
    else:
        kern = functools.partial(_ffn_expert_kernel, bm=bm, sb=sb)
    return pl.pallas_call(
        kern,
        grid_spec=pltpu.PrefetchScalarGridSpec(
            num_scalar_prefetch=2,
            grid=(nt, nf),
            in_specs=in_specs,
            out_specs=pl.BlockSpec((bm, d), lambda i, f, te, tv: (i, 0)),
            scratch_shapes=[pltpu.VMEM((bm, d), BF16), pltpu.VMEM((bm, d), F32)],
        ),
        out_shape=jax.ShapeDtypeStruct((r, d), F32),
        compiler_params=_cparams("parallel", "arbitrary"),
        name="swiglu_dense" if ln is not None else "swiglu_experts",
    )(tile_expert, tile_valid, *args)


def _router_kernel(x_ref, wh_ref, wl_ref, o_ref, *, ne):
    x = x_ref[...]
    xh = x.astype(BF16)
    xl = (x - xh.astype(F32)).astype(BF16)
    wh = wh_ref[...]
    lg = (jnp.dot(xh, wh, preferred_element_type=F32)
          + (jnp.dot(xl, wh, preferred_element_type=F32)
             + jnp.dot(xh, wl_ref[...], preferred_element_type=F32)))
    lane = lax.broadcasted_iota(jnp.int32, lg.shape, 1)
    lg = jnp.where(lane < ne, lg, -jnp.inf)
    m1 = jnp.max(lg, axis=-1, keepdims=True)
    i1 = jnp.min(jnp.where(lg == m1, lane, V7X_LANES), axis=-1, keepdims=True)
    lg2 = jnp.where(lane == i1, -jnp.inf, lg)
    m2 = jnp.max(lg2, axis=-1, keepdims=True)
    i2 = jnp.min(jnp.where(lg2 == m2, lane, V7X_LANES), axis=-1, keepdims=True)
    e = jnp.exp(m2 - m1)
    den = 1.0 + e
    g1 = 1.0 / den
    g2 = e / den
    out = jnp.where(lane == 0, i1.astype(F32),
                    jnp.where(lane == 1, i2.astype(F32),
                              jnp.where(lane == 2, g1, jnp.where(lane == 3, g2, 0.0))))
    o_ref[...] = out


def _router(x, wh, wl, *, ne):
    m, d = x.shape
    bm = _tile(m, 512)
    return pl.pallas_call(
        functools.partial(_router_kernel, ne=ne),
        grid=(m // bm,),
        in_specs=[
            pl.BlockSpec((bm, d), lambda i: (i, 0)),
            pl.BlockSpec(wh.shape, lambda i: (0, 0)),
            pl.BlockSpec(wl.shape, lambda i: (0, 0)),
        ],
        out_specs=pl.BlockSpec((bm, V7X_LANES), lambda i: (i, 0)),
        out_shape=jax.ShapeDtypeStruct((m, V7X_LANES), F32),
        compiler_params=_cparams("parallel"),
        name="router",
    )(x, wh, wl)


def _row_copy(src, src_row, dst, dst_row, sem):
    return pltpu.make_async_copy(src.at[pl.ds(src_row, 1)], dst.at[pl.ds(dst_row, 1)], sem)


def _dispatch_kernel(pos_ref, x_hbm, init_hbm, xs_hbm, sem, *, bt):
    del init_hbm
    base = pl.program_id(0) * bt

    def issue(r, carry):
        for k in range(TOP_K):
            _row_copy(x_hbm, base + r, xs_hbm, pos_ref[(base + r) * TOP_K + k], sem).start()
        return carry

    def drain(r, carry):
        for k in range(TOP_K):
            _row_copy(x_hbm, 0, xs_hbm, 0, sem).wait()
        return carry

    lax.fori_loop(0, bt, issue, 0)
    lax.fori_loop(0, bt, drain, 0)


def _dispatch(x, pos, rows):
    m, d = x.shape
    bt = _tile(m, 512)
    init = jnp.zeros((rows, d), x.dtype)
    return pl.pallas_call(
        functools.partial(_dispatch_kernel, bt=bt),
        grid_spec=pltpu.PrefetchScalarGridSpec(
            num_scalar_prefetch=1,
            grid=(m // bt,),
            in_specs=[pl.BlockSpec(memory_space=pl.ANY), pl.BlockSpec(memory_space=pl.ANY)],
            out_specs=pl.BlockSpec(memory_space=pl.ANY),
            scratch_shapes=[pltpu.SemaphoreType.DMA(())],
        ),
        out_shape=jax.ShapeDtypeStruct((rows, d), x.dtype),
        input_output_aliases={2: 0},
        compiler_params=pltpu.CompilerParams(dimension_semantics=("arbitrary",)),
        name="moe_dispatch",
    )(pos, x, init)


def _combine_kernel(pos_ref, ys_hbm, x_ref, rt_ref, g_ref, b_ref, o_ref, buf, sem, *, bt, alpha):
    base = pl.program_id(0) * bt

    def issue(r, carry):
        for k in range(TOP_K):
            _row_copy(ys_hbm, pos_ref[(base + r) * TOP_K + k], buf.at[k], r, sem).start()
        return carry

    def drain(r, carry):
        for k in range(TOP_K):
            _row_copy(ys_hbm, 0, buf.at[k], 0, sem).wait()
        return carry

    lax.fori_loop(0, bt, issue, 0)
    lax.fori_loop(0, bt, drain, 0)
    rt = rt_ref[...]
    y = buf[0] * rt[:, TOP_K:TOP_K + 1]
    for k in range(1, TOP_K):
        y = y + buf[k] * rt[:, TOP_K + k:TOP_K + k + 1]
    o_ref[...] = _layer_norm(alpha * x_ref[...] + y, g_ref[...], b_ref[...])


def _combine(ys, pos, x, route, g, b, *, alpha):
    m, d = x.shape
    bt = _tile(m, 256)
    row = lambda i, p: (i, 0)
    full = lambda i, p: (0, 0)
    return pl.pallas_call(
        functools.partial(_combine_kernel, bt=bt, alpha=alpha),
        grid_spec=pltpu.PrefetchScalarGridSpec(
            num_scalar_prefetch=1,
            grid=(m // bt,),
            in_specs=[
                pl.BlockSpec(memory_space=pl.ANY),
                pl.BlockSpec((bt, d), row),
                pl.BlockSpec((bt, V7X_LANES), row),
                pl.BlockSpec((1, d), full),
                pl.BlockSpec((1, d), full),
            ],
            out_specs=pl.BlockSpec((bt, d), row),
            scratch_shapes=[pltpu.VMEM((TOP_K, bt, d), F32), pltpu.SemaphoreType.DMA(())],
        ),
        out_shape=jax.ShapeDtypeStruct((m, d), F32),
        compiler_params=_cparams("arbitrary"),
        name="moe_combine",
    )(pos, ys, x, route, g, b)


def _moe_plan(idx, ne, bm, nt):
    e_flat = idx.reshape(-1)
    onehot = (e_flat[:, None] == jnp.arange(ne, dtype=jnp.int32)[None, :]).astype(jnp.int32)
    csum = jnp.cumsum(onehot, axis=0)
    rank = jnp.take_along_axis(csum, e_flat[:, None], axis=1)[:, 0] - 1
    counts = csum[-1]
    tiles_e = (counts + bm - 1) // bm
    tile_end = jnp.cumsum(tiles_e)
    tile_start = tile_end - tiles_e
    pos = (tile_start[e_flat] * bm + rank).astype(jnp.int32)
    t = jnp.arange(nt, dtype=jnp.int32)
    te = jnp.minimum(jnp.searchsorted(tile_end, t, side="right"), ne - 1).astype(jnp.int32)
    tv = jnp.clip(counts[te] - (t - tile_start[te]) * bm, 0, bm).astype(jnp.int32)
    return pos, te, tv


def _moe_layer(x, wr_hi, wr_lo, wg, wu, wd, g, b, *, ne, alpha):
    m, d = x.shape
    route = _router(x, wr_hi, wr_lo, ne=ne)
    idx = route[:, :TOP_K].astype(jnp.int32)
    bm = 512
    nt = (m * TOP_K) // bm + ne
    pos, te, tv = _moe_plan(idx, ne, bm, nt)
    xs = _dispatch(x, pos, nt * bm)
    ys = _ffn_call(xs, wg, wu, wd, te, tv, bm=bm)
    return _combine(ys, pos, x, route, g, b, alpha=alpha)


def _dense_ffn_layer(x, wg, wu, wd, g, b, *, alpha):
    m = x.shape[0]
    bm = _tile(m, 512)
    nt = m // bm
    te = jnp.zeros((nt,), jnp.int32)
    tv = jnp.full((nt,), bm, jnp.int32)
    return _ffn_call(x, wg[None], wu[None], wd[None], te, tv, bm=bm, ln=(g, b), alpha=alpha)


def _rope_tables(pos, rope, scale):
    inv = ROPE_BASE ** (-jnp.arange(0, rope, 2, dtype=F32) / rope)
    ang = pos.astype(F32)[:, None] * inv[None, :]
    cos, sin = jnp.cos(ang), jnp.sin(ang)
    t1 = jnp.concatenate([cos, cos, sin, sin], axis=1)
    ones = jnp.ones((pos.shape[0], V7X_LANES), F32)
    t2 = jnp.concatenate([ones, t1], axis=1) * scale
    return t1, t2


def _swap_halves(w):
    half = w.shape[-1] // 2
    return jnp.concatenate([-w[..., half:], w[..., :half]], axis=-1)


def _prep_mla(w_dq, w_uq, w_dkv, w_uk, w_uv, w_o, *, kl, heads, nope, rope):
    ql = w_dq.shape[1]
    kr = w_dkv[:, kl:]
    w1 = jnp.concatenate([w_dq, w_dkv[:, :kl], kr, _swap_halves(kr)], axis=1).astype(BF16)
    uq = w_uq.reshape(ql, heads, nope + rope)
    qr = uq[..., nope:]
    w2 = jnp.concatenate([uq[..., :nope], qr, _swap_halves(qr)], axis=-1)
    w2 = w2.reshape(ql, heads * (nope + 2 * rope)).astype(BF16)
    wk = w_uk.reshape(kl, -1).astype(BF16)
    wv = w_uv.reshape(kl, -1).astype(BF16)
    return w1, w2, wk, wv, w_o.astype(BF16)


def kernel(x_prompt, x_sample, cache_ckv, cache_krope, state_pool, mla_w_dq, mla_q_norm, mla_w_uq, mla_w_dkv, mla_kv_norm, mla_w_uk, mla_w_uv, mla_w_o, pool_w, pool_scale, ffn_w_gate, ffn_w_up, ffn_w_down, moe_w_router, moe_w_gate, moe_w_up, moe_w_down, ln_mix_g, ln_mix_b, ln_ffn_g, ln_ffn_b):
    bp, sp, d = x_prompt.shape
    bs, ss, _ = x_sample.shape
    past = cache_ckv.shape[2]
    depth = ln_mix_g.shape[0]
    kl, heads, nope = mla_w_uk.shape[1:]
    vh = mla_w_uv.shape[3]
    ql = mla_w_dq.shape[2]
    rope = mla_w_dkv.shape[2] - kl
    hd = nope + 2 * rope
    ne = moe_w_router.shape[2]
    alpha = (2.0 * depth) ** 0.25
    scale = 1.0 / math.sqrt(nope + rope)
    assert bp == 1 and heads % 2 == 0 and 2 * rope == V7X_LANES and nope == V7X_LANES
    assert past % CHUNK == 0 and state_pool.shape[2] == POOL_HALO - 1

    t1_p, t2_p = _rope_tables(jnp.arange(sp, dtype=jnp.int32), rope, scale)
    t1_s, t2_s = _rope_tables(jnp.tile(past + jnp.arange(ss, dtype=jnp.int32), bs), rope, scale)

    xp = x_prompt.reshape(sp, d)
    xs = x_sample.reshape(bs * ss, d)
    row = lambda v: v.reshape(1, -1)
    ckv_p, kr_p, pool_p, ckv_s, kr_s, pool_s = [], [], [], [], [], []
    blk = _tile(sp, 512)

    for i in range(depth):
        j = i // 2
        if i % 2 == 0:
            w1, w2, wk, wv, wo = _prep_mla(mla_w_dq[j], mla_w_uq[j], mla_w_dkv[j], mla_w_uk[j],
                                           mla_w_uv[j], mla_w_o[j], kl=kl, heads=heads, nope=nope, rope=rope)
            dims = dict(ql=ql, kl=kl, rope=rope, heads=heads, hd=hd)
            up = dict(kl=kl, heads=heads, nope=nope, vh=vh, hd=hd)
            qn, kvn = row(mla_q_norm[j]), row(mla_kv_norm[j])
            lg, lb = row(ln_mix_g[i]), row(ln_mix_b[i])
            q, ckv, kr, ckr = _mla_proj(xp, w1, w2, qn, kvn, t1_p, t2_p, **dims)
            k, v = _kv_up(ckr, wk, wv, **up)
            o = _attn_prompt(q, k, v, blk=blk)
            xp = _proj_res_ln(o, wo, xp, lg, lb, alpha=alpha)
            ckv_p.append(ckv.reshape(bp, sp, kl))
            kr_p.append(kr.reshape(bp, sp, rope))
            q, ckv, kr, ckr = _mla_proj(xs, w1, w2, qn, kvn, t1_s, t2_s, **dims)
            kn, vn = _kv_up(ckr, wk, wv, **up)
            ckr_c = jnp.concatenate([cache_ckv[j], cache_krope[j], cache_krope[j]], axis=-1)
            kc, vc = _kv_up(ckr_c.reshape(bs * past, kl + 2 * rope).astype(BF16), wk, wv, **up)
            o = _attn_sample(q, kc, vc, kn, vn, batch=bs, past=past)
            xs = _proj_res_ln(o, wo, xs, lg, lb, alpha=alpha)
            ckv_s.append(ckv.reshape(bs, ss, kl))
            kr_s.append(kr.reshape(bs, ss, rope))
        else:
            pw = pool_w[j].astype(BF16)
            sc = row(pool_scale[j])
            lg, lb = row(ln_mix_g[i]), row(ln_mix_b[i])
            xp3 = xp.reshape(bp, sp, d)
            xs3 = xs.reshape(bs, ss, d)
            pool_p.append(xp3[:, sp - (POOL_HALO - 1):, :])
            pool_s.append(jnp.concatenate([state_pool[j], xs3], axis=1)[:, ss:, :])
            hist_p = jnp.zeros((bp, POOL_HALO, d), F32)
            hist_s = jnp.concatenate([jnp.zeros((bs, 1, d), F32), state_pool[j]], axis=1)
            xp = _pool_layer(xp3, hist_p, pw, sc, lg, lb, pos0=0, alpha=alpha).reshape(sp, d)
            xs = _pool_layer(xs3, hist_s, pw, sc, lg, lb, pos0=past, alpha=alpha).reshape(bs * ss, d)

        fg, fb = row(ln_ffn_g[i]), row(ln_ffn_b[i])
        if i % 2 == 0:
            wg, wu, wd = (ffn_w_gate[j].astype(BF16), ffn_w_up[j].astype(BF16), ffn_w_down[j].astype(BF16))
            xp = _dense_ffn_layer(xp, wg, wu, wd, fg, fb, alpha=alpha)
            xs = _dense_ffn_layer(xs, wg, wu, wd, fg, fb, alpha=alpha)
        else:
            wr = jnp.pad(moe_w_router[j], ((0, 0), (0, V7X_LANES - ne)))
            wr_hi = wr.astype(BF16)
            wr_lo = (wr - wr_hi.astype(F32)).astype(BF16)
            wg, wu, wd = (moe_w_gate[j].astype(BF16), moe_w_up[j].astype(BF16), moe_w_down[j].astype(BF16))
            xp = _moe_layer(xp, wr_hi, wr_lo, wg, wu, wd, fg, fb, ne=ne, alpha=alpha)
            xs = _moe_layer(xs, wr_hi, wr_lo, wg, wu, wd, fg, fb, ne=ne, alpha=alpha)

    return (xp.reshape(bp, sp, d), xs.reshape(bs, ss, d),
            jnp.stack(ckv_p), jnp.stack(kr_p), jnp.stack(pool_p),
            jnp.stack(ckv_s), jnp.stack(kr_s), jnp.stack(pool_s))
```

```python
import functools
import math

import jax
import jax.numpy as jnp
from jax import lax
from jax.experimental import pallas as pl
from jax.experimental.pallas import tpu as pltpu

BF16 = jnp.bfloat16
F32 = jnp.float32

CHUNK = 64
POOL_WINDOWS = (2, 4, 8, 16)
POOL_HALO = 16
ROPE_BASE = 10000.0
LN_EPS = 1e-5
RMS_EPS = 1e-6
TOP_K = 2

V7X_LANES = 128
V7X_VMEM_BUDGET = 56 * 1024 * 1024


def _cparams(*sem):
    return pltpu.CompilerParams(dimension_semantics=sem, vmem_limit_bytes=V7X_VMEM_BUDGET)


def _tile(n, pref):
    if n <= pref:
        return n
    t = pref
    while n % t:
        t //= 2
    assert t >= 8, (n, pref)
    return t


def _layer_norm(z, g, b):
    mu = jnp.mean(z, axis=-1, keepdims=True)
    zc = z - mu
    var = jnp.mean(zc * zc, axis=-1, keepdims=True)
    return zc * lax.rsqrt(var + LN_EPS) * g + b


def _rms(c, g):
    return c * lax.rsqrt(jnp.mean(c * c, axis=-1, keepdims=True) + RMS_EPS) * g


def _mla_proj_kernel(x_ref, w1_ref, w2_ref, qn_ref, kvn_ref, t1_ref, t2_ref,
                     q_ref, ckv_ref, kr_ref, ckr_ref, *, ql, kl, rope, heads, hd):
    xb = x_ref[...].astype(BF16)
    r = jnp.dot(xb, w1_ref[...], preferred_element_type=F32)
    cq = _rms(r[:, :ql], qn_ref[...])
    ckv = _rms(r[:, ql:ql + kl], kvn_ref[...])
    t = r[:, ql + kl:] * t1_ref[...]
    kk = t + pltpu.roll(t, rope, axis=1)
    ckv_ref[...] = ckv
    kr_ref[...] = kk[:, :rope]
    ckr_ref[:, :kl] = ckv.astype(BF16)
    ckr_ref[:, kl:] = kk.astype(BF16)
    cqb = cq.astype(BF16)
    t2 = t2_ref[...]
    for h in range(heads):
        qh = jnp.dot(cqb, w2_ref[:, h * hd:(h + 1) * hd], preferred_element_type=F32)
        q_ref[h] = (qh * t2).astype(BF16)


def _mla_proj(x, w1, w2, qn, kvn, t1, t2, *, ql, kl, rope, heads, hd):
    m, d = x.shape
    bm = _tile(m, 256)
    kern = functools.partial(_mla_proj_kernel, ql=ql, kl=kl, rope=rope, heads=heads, hd=hd)
    row = lambda i: (i, 0)
    full = lambda i: (0, 0)
    return pl.pallas_call(
        kern,
        grid=(m // bm,),
        in_specs=[
            pl.BlockSpec((bm, d), row),
            pl.BlockSpec(w1.shape, full),
            pl.BlockSpec(w2.shape, full),
            pl.BlockSpec((1, ql), full),
            pl.BlockSpec((1, kl), full),
            pl.BlockSpec((bm, 2 * rope), row),
            pl.BlockSpec((bm, hd), row),
        ],
        out_specs=[
            pl.BlockSpec((heads, bm, hd), lambda i: (0, i, 0)),
            pl.BlockSpec((bm, kl), row),
            pl.BlockSpec((bm, rope), row),
            pl.BlockSpec((bm, kl + 2 * rope), row),
        ],
        out_shape=[
            jax.ShapeDtypeStruct((heads, m, hd), BF16),
            jax.ShapeDtypeStruct((m, kl), F32),
            jax.ShapeDtypeStruct((m, rope), F32),
            jax.ShapeDtypeStruct((m, kl + 2 * rope), BF16),
        ],
        compiler_params=_cparams("parallel"),
        name="mla_proj",
    )(x, w1, w2, qn, kvn, t1, t2)


def _kv_up_kernel(ckr_ref, wk_ref, wv_ref, k_ref, v_ref, *, kl, heads, nope, vh):
    c = ckr_ref[:, :kl]
    kk = ckr_ref[:, kl:]
    for p in range(heads // 2):
        kn = jnp.dot(c, wk_ref[:, 2 * p * nope:(2 * p + 2) * nope], preferred_element_type=F32)
        vv = jnp.dot(c, wv_ref[:, 2 * p * vh:(2 * p + 2) * vh], preferred_element_type=F32)
        for s in range(2):
            h = 2 * p + s
            k_ref[h, :, :nope] = kn[:, s * nope:(s + 1) * nope].astype(BF16)
            k_ref[h, :, nope:] = kk
            v_ref[h] = vv[:, s * vh:(s + 1) * vh].astype(BF16)


def _kv_up(ckr, wk, wv, *, kl, heads, nope, vh, hd):
    m = ckr.shape[0]
    bm = _tile(m, 512)
    kern = functools.partial(_kv_up_kernel, kl=kl, heads=heads, nope=nope, vh=vh)
    return pl.pallas_call(
        kern,
        grid=(m // bm,),
        in_specs=[
            pl.BlockSpec((bm, ckr.shape[1]), lambda i: (i, 0)),
            pl.BlockSpec(wk.shape, lambda i: (0, 0)),
            pl.BlockSpec(wv.shape, lambda i: (0, 0)),
        ],
        out_specs=[
            pl.BlockSpec((heads, bm, hd), lambda i: (0, i, 0)),
            pl.BlockSpec((heads, bm, vh), lambda i: (0, i, 0)),
        ],
        out_shape=[
            jax.ShapeDtypeStruct((heads, m, hd), BF16),
            jax.ShapeDtypeStruct((heads, m, vh), BF16),
        ],
        compiler_params=_cparams("parallel"),
        name="kv_up",
    )(ckr, wk, wv)


def _attn_prompt_kernel(q_ref, k_ref, v_ref, o_ref, m_scr, l_scr, acc_scr, *, blk, hp, vh):
    i = pl.program_id(1)
    nc = blk // V7X_LANES
    m_scr[...] = jnp.full(m_scr.shape, -jnp.inf, F32)
    l_scr[...] = jnp.zeros(l_scr.shape, F32)
    acc_scr[...] = jnp.zeros(acc_scr.shape, F32)

    def block(j, masked):
        start = pl.multiple_of(j * blk, blk)
        for a in range(hp):
            k = k_ref[a, pl.ds(start, blk), :]
            v = v_ref[a, pl.ds(start, blk), :]
            s = lax.dot_general(q_ref[a], k, (((1,), (1,)), ((), ())), preferred_element_type=F32)
            if masked:
                qc = lax.broadcasted_iota(jnp.int32, s.shape, 0) // CHUNK
                kc = lax.broadcasted_iota(jnp.int32, s.shape, 1) // CHUNK
                s = jnp.where(kc <= qc, s, -jnp.inf)
            cols = [s[:, c * V7X_LANES:(c + 1) * V7X_LANES] for c in range(nc)]
            mx = cols[0]
            for c in cols[1:]:
                mx = jnp.maximum(mx, c)
            m_prev = m_scr[a]
            m_new = jnp.maximum(m_prev, jnp.max(mx, axis=-1, keepdims=True))
            alpha = jnp.exp2(m_prev - m_new)
            ps = [jnp.exp2(c - m_new) for c in cols]
            lsum = ps[0]
            for pc in ps[1:]:
                lsum = lsum + pc
            l_scr[a] = alpha * l_scr[a] + lsum
            p = jnp.concatenate([pc.astype(BF16) for pc in ps], axis=1)
            acc_scr[a] = alpha * acc_scr[a] + jnp.dot(p, v, preferred_element_type=F32)
            m_scr[a] = m_new

    def pair(j, carry):
        block(2 * j, False)
        block(2 * j + 1, False)
        return carry

    lax.fori_loop(0, i // 2, pair, 0)

    @pl.when(i % 2 == 1)
    def _():
        block(i - 1, False)

    block(i, True)
    for a in range(hp):
        l = jnp.sum(l_scr[a], axis=-1, keepdims=True)
        o_ref[:, a * vh:(a + 1) * vh] = (acc_scr[a] / l).astype(o_ref.dtype)


def _attn_prompt(q, k, v, *, blk, hp=2):
    heads, s, hd = q.shape
    vh = v.shape[2]
    assert heads % hp == 0 and vh == V7X_LANES
    kern = functools.partial(_attn_prompt_kernel, blk=blk, hp=hp, vh=vh)
    once = pl.Buffered(1)
    return pl.pallas_call(
        kern,
        grid=(heads // hp, s // blk),
        in_specs=[
            pl.BlockSpec((hp, blk, hd), lambda h, i: (h, i, 0)),
            pl.BlockSpec((hp, s, hd), lambda h, i: (h, 0, 0), pipeline_mode=once),
            pl.BlockSpec((hp, s, vh), lambda h, i: (h, 0, 0), pipeline_mode=once),
        ],
        out_specs=pl.BlockSpec((blk, hp * vh), lambda h, i: (i, h)),
        out_shape=jax.ShapeDtypeStruct((s, heads * vh), BF16),
        scratch_shapes=[
            pltpu.VMEM((hp, blk, V7X_LANES), F32),
            pltpu.VMEM((hp, blk, V7X_LANES), F32),
            pltpu.VMEM((hp, blk, vh), F32),
        ],
        compiler_params=_cparams("parallel", "arbitrary"),
        name="attn_prompt",
    )(q, k, v)


def _attn_sample_kernel(q_ref, kc_ref, vc_ref, kn_ref, vn_ref, o_ref, *, past):
    q = q_ref[...]
    dn = (((1,), (1,)), ((), ()))
    s1 = lax.dot_general(q, kc_ref[...], dn, preferred_element_type=F32)
    s2 = lax.dot_general(q, kn_ref[...], dn, preferred_element_type=F32)
    qc1 = (past + lax.broadcasted_iota(jnp.int32, s1.shape, 0)) // CHUNK
    kc1 = lax.broadcasted_iota(jnp.int32, s1.shape, 1) // CHUNK
    s1 = jnp.where(kc1 <= qc1, s1, -jnp.inf)
    qc2 = (past + lax.broadcasted_iota(jnp.int32, s2.shape, 0)) // CHUNK
    kc2 = (past + lax.broadcasted_iota(jnp.int32, s2.shape, 1)) // CHUNK
    s2 = jnp.where(kc2 <= qc2, s2, -jnp.inf)
    m = jnp.maximum(jnp.max(s1, axis=-1, keepdims=True), jnp.max(s2, axis=-1, keepdims=True))
    p1 = jnp.exp2(s1 - m)
    p2 = jnp.exp2(s2 - m)
    l = jnp.sum(p1, axis=-1, keepdims=True) + jnp.sum(p2, axis=-1, keepdims=True)
    o = (jnp.dot(p1.astype(BF16), vc_ref[...], preferred_element_type=F32)
         + jnp.dot(p2.astype(BF16), vn_ref[...], preferred_element_type=F32))
    o_ref[...] = (o / l).astype(o_ref.dtype)


def _attn_sample(q, kc, vc, kn, vn, *, batch, past):
    heads, m, hd = q.shape
    sq = m // batch
    vh = vc.shape[2]
    kern = functools.partial(_attn_sample_kernel, past=past)
    return pl.pallas_call(
        kern,
        grid=(batch, heads),
        in_specs=[
            pl.BlockSpec((None, sq, hd), lambda b, h: (h, b, 0)),
            pl.BlockSpec((None, past, hd), lambda b, h: (h, b, 0)),
            pl.BlockSpec((None, past, vh), lambda b, h: (h, b, 0)),
            pl.BlockSpec((None, sq, hd), lambda b, h: (h, b, 0)),
            pl.BlockSpec((None, sq, vh), lambda b, h: (h, b, 0)),
        ],
        out_specs=pl.BlockSpec((sq, vh), lambda b, h: (b, h)),
        out_shape=jax.ShapeDtypeStruct((m, heads * vh), BF16),
        compiler_params=_cparams("parallel", "parallel"),
        name="attn_sample",
    )(q, kc, vc, kn, vn)


def _proj_res_ln_kernel(a_ref, w_ref, x_ref, g_ref, b_ref, o_ref, *, alpha):
    y = jnp.dot(a_ref[...], w_ref[...], preferred_element_type=F32)
    o_ref[...] = _layer_norm(alpha * x_ref[...] + y, g_ref[...], b_ref[...])


def _proj_res_ln(a, w, x, g, b, *, alpha):
    m, k = a.shape
    d = w.shape[1]
    bm = _tile(m, 512)
    row = lambda i: (i, 0)
    full = lambda i: (0, 0)
    return pl.pallas_call(
        functools.partial(_proj_res_ln_kernel, alpha=alpha),
        grid=(m // bm,),
        in_specs=[
            pl.BlockSpec((bm, k), row),
            pl.BlockSpec((k, d), full),
            pl.BlockSpec((bm, d), row),
            pl.BlockSpec((1, d), full),
            pl.BlockSpec((1, d), full),
        ],
        out_specs=pl.BlockSpec((bm, d), row),
        out_shape=jax.ShapeDtypeStruct((m, d), F32),
        compiler_params=_cparams("parallel"),
        name="proj_res_ln",
    )(a, w, x, g, b)


def _pool_kernel(x_ref, prev_ref, hist_ref, w_ref, sc_ref, g_ref, b_ref, o_ref, buf, y_scr,
                 *, bm, gw, pos0, alpha):
    i = pl.program_id(1)
    x = x_ref[...]
    halo = jnp.where(i == 0, hist_ref[...], prev_ref[...])
    buf[0:POOL_HALO, :] = halo
    buf[POOL_HALO:POOL_HALO + bm, :] = x
    pos = pos0 + i * bm + lax.broadcasted_iota(jnp.int32, (bm, 1), 0)
    for gi, w in enumerate(POOL_WINDOWS):
        lo, hi = gi * gw, (gi + 1) * gw
        tot = x[:, lo:hi]
        for j in range(1, w):
            tot = tot + buf[POOL_HALO - j:POOL_HALO - j + bm, lo:hi]
        cnt = jnp.minimum(pos + 1, w).astype(F32)
        dlt = (tot / cnt - x[:, lo:hi]).astype(BF16)
        y_scr[:, lo:hi] = jnp.dot(dlt, w_ref[gi], preferred_element_type=F32)
    y = y_scr[...] * sc_ref[...]
    o_ref[...] = _layer_norm(alpha * x + y, g_ref[...], b_ref[...])


def _pool_layer(x, hist, w, sc, g, b, *, pos0, alpha):
    bsz, s, d = x.shape
    gw = d // len(POOL_WINDOWS)
    bm = _tile(s, 512)
    assert bm % POOL_HALO == 0
    hb = bm // POOL_HALO
    kern = functools.partial(_pool_kernel, bm=bm, gw=gw, pos0=pos0, alpha=alpha)
    vec = lambda bb, i: (0, 0)
    return pl.pallas_call(
        kern,
        grid=(bsz, s // bm),
        in_specs=[
            pl.BlockSpec((None, bm, d), lambda bb, i: (bb, i, 0)),
            pl.BlockSpec((None, POOL_HALO, d), lambda bb, i: (bb, jnp.maximum(i * hb - 1, 0), 0)),
            pl.BlockSpec((None, POOL_HALO, d), lambda bb, i: (bb, 0, 0)),
            pl.BlockSpec(w.shape, lambda bb, i: (0, 0, 0)),
            pl.BlockSpec((1, d), vec),
            pl.BlockSpec((1, d), vec),
            pl.BlockSpec((1, d), vec),
        ],
        out_specs=pl.BlockSpec((None, bm, d), lambda bb, i: (bb, i, 0)),
        out_shape=jax.ShapeDtypeStruct((bsz, s, d), F32),
        scratch_shapes=[
            pltpu.VMEM((bm + POOL_HALO, d), F32),
            pltpu.VMEM((bm, d), F32),
        ],
        compiler_params=_cparams("parallel", "arbitrary"),
        name="pool_mix",
    )(x, x, hist, w, sc, g, b)


def _swiglu_tile(tv_ref, x_ref, wg_ref, wu_ref, wd_ref, o_ref, xb_scr, wgb, wub, wdb, *, bm, sb):
    i = pl.program_id(0)
    f = pl.program_id(1)
    valid = tv_ref[i]

    @pl.when(f == 0)
    def _():
        o_ref[...] = jnp.zeros(o_ref.shape, F32)
        xb_scr[...] = x_ref[...].astype(BF16)

    def rows(lo, hi, wg, wu, wd):
        xs = xb_scr[lo:hi, :]
        gte = jnp.dot(xs, wg, preferred_element_type=F32)
        up = jnp.dot(xs, wu, preferred_element_type=F32)
        hid = (gte * jax.nn.sigmoid(gte) * up).astype(BF16)
        o_ref[lo:hi, :] += jnp.dot(hid, wd, preferred_element_type=F32)

    @pl.when(valid == bm)
    def _():
        rows(0, bm, wg_ref[...].astype(BF16), wu_ref[...].astype(BF16), wd_ref[...].astype(BF16))

    @pl.when((valid > 0) & (valid < bm))
    def _():
        wgb[...] = wg_ref[...].astype(BF16)
        wub[...] = wu_ref[...].astype(BF16)
        wdb[...] = wd_ref[...].astype(BF16)

    for s in range(bm // sb):
        @pl.when((s * sb < valid) & (valid < bm))
        def _():
            rows(s * sb, (s + 1) * sb, wgb[...], wub[...], wdb[...])


def _ffn_dense_kernel(te_ref, tv_ref, x_ref, wg_ref, wu_ref, wd_ref, g_ref, b_ref, o_ref,
                      xb_scr, wgb, wub, wdb, *, bm, sb, alpha):
    del te_ref
    _swiglu_tile(tv_ref, x_ref, wg_ref, wu_ref, wd_ref, o_ref, xb_scr, wgb, wub, wdb, bm=bm, sb=sb)

    @pl.when(pl.program_id(1) == pl.num_programs(1) - 1)
    def _():
        o_ref[...] = _layer_norm(alpha * x_ref[...] + o_ref[...], g_ref[...], b_ref[...])


def _ffn_expert_kernel(te_ref, tv_ref, x_ref, wg_ref, wu_ref, wd_ref, o_ref,
                       xb_scr, wgb, wub, wdb, *, bm, sb):
    del te_ref
    _swiglu_tile(tv_ref, x_ref, wg_ref, wu_ref, wd_ref, o_ref, xb_scr, wgb, wub, wdb, bm=bm, sb=sb)


def _ffn_call(x, wg, wu, wd, layer, tile_expert, tile_valid, *, bm, ln=None, alpha=None):
    r, d = x.shape
    nf_total = wg.shape[3]
    bf = _tile(nf_total, 256)
    nf = nf_total // bf
    sb = _tile(bm, 256)
    nt = r // bm

    def chunk(i, f, te, tv):
        return jnp.where(tv[i] > 0, f, nf - 1)

    in_specs = [
        pl.BlockSpec((bm, d), lambda i, f, te, tv: (i, 0), pipeline_mode=pl.Buffered(1)),
        pl.BlockSpec((None, None, d, bf), lambda i, f, te, tv: (layer, te[i], 0, chunk(i, f, te, tv))),
        pl.BlockSpec((None, None, d, bf), lambda i, f, te, tv: (layer, te[i], 0, chunk(i, f, te, tv))),
        pl.BlockSpec((None, None, bf, d), lambda i, f, te, tv: (layer, te[i], chunk(i, f, te, tv), 0)),
    ]
    args = [x, wg, wu, wd]
    if ln is not None:
        in_specs += [pl.BlockSpec((1, d), lambda i, f, te, tv: (0, 0))] * 2
        args += list(ln)
        kern = functools.partial(_ffn_dense_kernel, bm=bm, sb=sb, alpha=alpha)
    else:
        kern = functools.partial(_ffn_expert_kernel, bm=bm, sb=sb)
    return pl.pallas_call(
        kern,
        grid_spec=pltpu.PrefetchScalarGridSpec(
            num_scalar_prefetch=2,
            grid=(nt, nf),
            in_specs=in_specs,
            out_specs=pl.BlockSpec((bm, d), lambda i, f, te, tv: (i, 0)),
            scratch_shapes=[pltpu.VMEM((bm, d), BF16), pltpu.VMEM((d, bf), BF16),
                            pltpu.VMEM((d, bf), BF16), pltpu.VMEM((bf, d), BF16)],
        ),
        out_shape=jax.ShapeDtypeStruct((r, d), F32),
        compiler_params=_cparams("parallel", "arbitrary"),
        name="swiglu_dense" if ln is not None else "swiglu_experts",
    )(tile_expert, tile_valid, *args)


def _router_kernel(x_ref, wh_ref, wl_ref, o_ref, *, ne):
    x = x_ref[...]
    xh = x.astype(BF16)
    xl = (x - xh.astype(F32)).astype(BF16)
    wh = wh_ref[...]
    lg = (jnp.dot(xh, wh, preferred_element_type=F32)
          + (jnp.dot(xl, wh, preferred_element_type=F32)
             + jnp.dot(xh, wl_ref[...], preferred_element_type=F32)))
    lane = lax.broadcasted_iota(jnp.int32, lg.shape, 1)
    lg = jnp.where(lane < ne, lg, -jnp.inf)
    m1 = jnp.max(lg, axis=-1, keepdims=True)
    i1 = jnp.min(jnp.where(lg == m1, lane, V7X_LANES), axis=-1, keepdims=True)
    lg2 = jnp.where(lane == i1, -jnp.inf, lg)
    m2 = jnp.max(lg2, axis=-1, keepdims=True)
    i2 = jnp.min(jnp.where(lg2 == m2, lane, V7X_LANES), axis=-1, keepdims=True)
    e = jnp.exp(m2 - m1)
    den = 1.0 + e
    g1 = 1.0 / den
    g2 = e / den
    out = jnp.where(lane == 0, i1.astype(F32),
                    jnp.where(lane == 1, i2.astype(F32),
                              jnp.where(lane == 2, g1, jnp.where(lane == 3, g2, 0.0))))
    o_ref[...] = out


def _router(x, wh, wl, *, ne):
    m, d = x.shape
    bm = _tile(m, 512)
    return pl.pallas_call(
        functools.partial(_router_kernel, ne=ne),
        grid=(m // bm,),
        in_specs=[
            pl.BlockSpec((bm, d), lambda i: (i, 0)),
            pl.BlockSpec(wh.shape, lambda i: (0, 0)),
            pl.BlockSpec(wl.shape, lambda i: (0, 0)),
        ],
        out_specs=pl.BlockSpec((bm, V7X_LANES), lambda i: (i, 0)),
        out_shape=jax.ShapeDtypeStruct((m, V7X_LANES), F32),
        compiler_params=_cparams("parallel"),
        name="router",
    )(x, wh, wl)


def _row_copy(src, src_row, dst, dst_row, sem):
    return pltpu.make_async_copy(src.at[pl.ds(src_row, 1)], dst.at[pl.ds(dst_row, 1)], sem)


def _dispatch_kernel(pos_ref, x_ref, init_hbm, xs_hbm, sem, *, bt):
    del init_hbm
    base = pl.program_id(0) * bt

    def issue(r, carry):
        for k in range(TOP_K):
            _row_copy(x_ref, r, xs_hbm, pos_ref[(base + r) * TOP_K + k], sem).start()
        return carry

    def drain(r, carry):
        for k in range(TOP_K):
            _row_copy(x_ref, 0, xs_hbm, 0, sem).wait()
        return carry

    lax.fori_loop(0, bt, issue, 0)
    lax.fori_loop(0, bt, drain, 0)


def _dispatch(x, pos, rows):
    m, d = x.shape
    bt = _tile(m, 256)
    init = jnp.zeros((rows, d), x.dtype)
    return pl.pallas_call(
        functools.partial(_dispatch_kernel, bt=bt),
        grid_spec=pltpu.PrefetchScalarGridSpec(
            num_scalar_prefetch=1,
            grid=(m // bt,),
            in_specs=[pl.BlockSpec((bt, d), lambda i, p: (i, 0)), pl.BlockSpec(memory_space=pl.ANY)],
            out_specs=pl.BlockSpec(memory_space=pl.ANY),
            scratch_shapes=[pltpu.SemaphoreType.DMA(())],
        ),
        out_shape=jax.ShapeDtypeStruct((rows, d), x.dtype),
        input_output_aliases={2: 0},
        compiler_params=_cparams("arbitrary"),
        name="moe_dispatch",
    )(pos, x, init)


def _combine_kernel(pos_ref, ys_hbm, x_ref, rt_ref, g_ref, b_ref, o_ref, buf, sem, *, bt, alpha):
    base = pl.program_id(0) * bt

    def issue(r, carry):
        for k in range(TOP_K):
            _row_copy(ys_hbm, pos_ref[(base + r) * TOP_K + k], buf.at[k], r, sem).start()
        return carry

    def drain(r, carry):
        for k in range(TOP_K):
            _row_copy(ys_hbm, 0, buf.at[k], 0, sem).wait()
        return carry

    lax.fori_loop(0, bt, issue, 0)
    lax.fori_loop(0, bt, drain, 0)
    rt = rt_ref[...]
    y = buf[0] * rt[:, TOP_K:TOP_K + 1]
    for k in range(1, TOP_K):
        y = y + buf[k] * rt[:, TOP_K + k:TOP_K + k + 1]
    o_ref[...] = _layer_norm(alpha * x_ref[...] + y, g_ref[...], b_ref[...])


def _combine(ys, pos, x, route, g, b, *, alpha):
    m, d = x.shape
    bt = _tile(m, 256)
    row = lambda i, p: (i, 0)
    full = lambda i, p: (0, 0)
    return pl.pallas_call(
        functools.partial(_combine_kernel, bt=bt, alpha=alpha),
        grid_spec=pltpu.PrefetchScalarGridSpec(
            num_scalar_prefetch=1,
            grid=(m // bt,),
            in_specs=[
                pl.BlockSpec(memory_space=pl.ANY),
                pl.BlockSpec((bt, d), row),
                pl.BlockSpec((bt, V7X_LANES), row),
                pl.BlockSpec((1, d), full),
                pl.BlockSpec((1, d), full),
            ],
            out_specs=pl.BlockSpec((bt, d), row),
            scratch_shapes=[pltpu.VMEM((TOP_K, bt, d), F32), pltpu.SemaphoreType.DMA(())],
        ),
        out_shape=jax.ShapeDtypeStruct((m, d), F32),
        compiler_params=_cparams("arbitrary"),
        name="moe_combine",
    )(pos, ys, x, route, g, b)


def _moe_plan(idx, ne, bm, nt):
    e_flat = idx.reshape(-1)
    onehot = (e_flat[:, None] == jnp.arange(ne, dtype=jnp.int32)[None, :]).astype(jnp.int32)
    csum = jnp.cumsum(onehot, axis=0)
    rank = jnp.take_along_axis(csum, e_flat[:, None], axis=1)[:, 0] - 1
    counts = csum[-1]
    tiles_e = (counts + bm - 1) // bm
    tile_end = jnp.cumsum(tiles_e)
    tile_start = tile_end - tiles_e
    pos = (tile_start[e_flat] * bm + rank).astype(jnp.int32)
    t = jnp.arange(nt, dtype=jnp.int32)
    te = jnp.minimum(jnp.sum((t[:, None] >= tile_end[None, :]).astype(jnp.int32), axis=1), ne - 1)
    tv = jnp.clip(counts[te] - (t - tile_start[te]) * bm, 0, bm).astype(jnp.int32)
    return pos, te, tv


FFN_ROW_TILE = 1024


def _moe_layer(x, wr_hi, wr_lo, wg, wu, wd, layer, g, b, *, ne, alpha):
    m, d = x.shape
    route = _router(x, wr_hi, wr_lo, ne=ne)
    idx = route[:, :TOP_K].astype(jnp.int32)
    bm = FFN_ROW_TILE
    nt = (m * TOP_K) // bm + ne
    pos, te, tv = _moe_plan(idx, ne, bm, nt)
    xs = _dispatch(x, pos, nt * bm)
    ys = _ffn_call(xs, wg, wu, wd, layer, te, tv, bm=bm)
    return _combine(ys, pos, x, route, g, b, alpha=alpha)


def _dense_ffn_layer(x, wg, wu, wd, layer, g, b, *, alpha):
    m = x.shape[0]
    bm = _tile(m, FFN_ROW_TILE)
    nt = m // bm
    te = jnp.zeros((nt,), jnp.int32)
    tv = jnp.full((nt,), bm, jnp.int32)
    return _ffn_call(x, wg[:, None], wu[:, None], wd[:, None], layer, te, tv,
                     bm=bm, ln=(g, b), alpha=alpha)


def _rope_tables(pos, rope, scale):
    inv = ROPE_BASE ** (-jnp.arange(0, rope, 2, dtype=F32) / rope)
    ang = pos.astype(F32)[:, None] * inv[None, :]
    cos, sin = jnp.cos(ang), jnp.sin(ang)
    t1 = jnp.concatenate([cos, cos, sin, sin], axis=1)
    ones = jnp.ones((pos.shape[0], V7X_LANES), F32)
    t2 = jnp.concatenate([ones, t1], axis=1) * scale
    return t1, t2


def _swap_halves(w):
    half = w.shape[-1] // 2
    return jnp.concatenate([-w[..., half:], w[..., :half]], axis=-1)


def _prep_mla(w_dq, w_uq, w_dkv, w_uk, w_uv, w_o, *, kl, heads, nope, rope):
    ql = w_dq.shape[1]
    kr = w_dkv[:, kl:]
    w1 = jnp.concatenate([w_dq, w_dkv[:, :kl], kr, _swap_halves(kr)], axis=1).astype(BF16)
    uq = w_uq.reshape(ql, heads, nope + rope)
    qr = uq[..., nope:]
    w2 = jnp.concatenate([uq[..., :nope], qr, _swap_halves(qr)], axis=-1)
    w2 = w2.reshape(ql, heads * (nope + 2 * rope)).astype(BF16)
    wk = w_uk.reshape(kl, -1).astype(BF16)
    wv = w_uv.reshape(kl, -1).astype(BF16)
    return w1, w2, wk, wv, w_o.astype(BF16)


def kernel(x_prompt, x_sample, cache_ckv, cache_krope, state_pool, mla_w_dq, mla_q_norm, mla_w_uq, mla_w_dkv, mla_kv_norm, mla_w_uk, mla_w_uv, mla_w_o, pool_w, pool_scale, ffn_w_gate, ffn_w_up, ffn_w_down, moe_w_router, moe_w_gate, moe_w_up, moe_w_down, ln_mix_g, ln_mix_b, ln_ffn_g, ln_ffn_b):
    bp, sp, d = x_prompt.shape
    bs, ss, _ = x_sample.shape
    past = cache_ckv.shape[2]
    depth = ln_mix_g.shape[0]
    kl, heads, nope = mla_w_uk.shape[1:]
    vh = mla_w_uv.shape[3]
    ql = mla_w_dq.shape[2]
    rope = mla_w_dkv.shape[2] - kl
    hd = nope + 2 * rope
    ne = moe_w_router.shape[2]
    alpha = (2.0 * depth) ** 0.25
    scale = math.log2(math.e) / math.sqrt(nope + rope)
    assert bp == 1 and heads % 2 == 0 and 2 * rope == V7X_LANES and nope == V7X_LANES
    assert past % CHUNK == 0 and state_pool.shape[2] == POOL_HALO - 1

    t1_p, t2_p = _rope_tables(jnp.arange(sp, dtype=jnp.int32), rope, scale)
    t1_s, t2_s = _rope_tables(jnp.tile(past + jnp.arange(ss, dtype=jnp.int32), bs), rope, scale)

    xp = x_prompt.reshape(sp, d)
    xs = x_sample.reshape(bs * ss, d)
    row = lambda v: v.reshape(1, -1)
    ckv_p, kr_p, pool_p, ckv_s, kr_s, pool_s = [], [], [], [], [], []
    blk = _tile(sp, 512)

    for i in range(depth):
        j = i // 2
        if i % 2 == 0:
            w1, w2, wk, wv, wo = _prep_mla(mla_w_dq[j], mla_w_uq[j], mla_w_dkv[j], mla_w_uk[j],
                                           mla_w_uv[j], mla_w_o[j], kl=kl, heads=heads, nope=nope, rope=rope)
            dims = dict(ql=ql, kl=kl, rope=rope, heads=heads, hd=hd)
            up = dict(kl=kl, heads=heads, nope=nope, vh=vh, hd=hd)
            qn, kvn = row(mla_q_norm[j]), row(mla_kv_norm[j])
            lg, lb = row(ln_mix_g[i]), row(ln_mix_b[i])
            q, ckv, kr, ckr = _mla_proj(xp, w1, w2, qn, kvn, t1_p, t2_p, **dims)
            k, v = _kv_up(ckr, wk, wv, **up)
            o = _attn_prompt(q, k, v, blk=blk)
            xp = _proj_res_ln(o, wo, xp, lg, lb, alpha=alpha)
            ckv_p.append(ckv.reshape(bp, sp, kl))
            kr_p.append(kr.reshape(bp, sp, rope))
            q, ckv, kr, ckr = _mla_proj(xs, w1, w2, qn, kvn, t1_s, t2_s, **dims)
            kn, vn = _kv_up(ckr, wk, wv, **up)
            ckr_c = jnp.concatenate([cache_ckv[j], cache_krope[j], cache_krope[j]], axis=-1)
            kc, vc = _kv_up(ckr_c.reshape(bs * past, kl + 2 * rope).astype(BF16), wk, wv, **up)
            o = _attn_sample(q, kc, vc, kn, vn, batch=bs, past=past)
            xs = _proj_res_ln(o, wo, xs, lg, lb, alpha=alpha)
            ckv_s.append(ckv.reshape(bs, ss, kl))
            kr_s.append(kr.reshape(bs, ss, rope))
        else:
            pw = pool_w[j].astype(BF16)
            sc = row(pool_scale[j])
            lg, lb = row(ln_mix_g[i]), row(ln_mix_b[i])
            xp3 = xp.reshape(bp, sp, d)
            xs3 = xs.reshape(bs, ss, d)
            pool_p.append(xp3[:, sp - (POOL_HALO - 1):, :])
            pool_s.append(jnp.concatenate([state_pool[j], xs3], axis=1)[:, ss:, :])
            hist_p = jnp.zeros((bp, POOL_HALO, d), F32)
            hist_s = jnp.concatenate([jnp.zeros((bs, 1, d), F32), state_pool[j]], axis=1)
            xp = _pool_layer(xp3, hist_p, pw, sc, lg, lb, pos0=0, alpha=alpha).reshape(sp, d)
            xs = _pool_layer(xs3, hist_s, pw, sc, lg, lb, pos0=past, alpha=alpha).reshape(bs * ss, d)

        fg, fb = row(ln_ffn_g[i]), row(ln_ffn_b[i])
        if i % 2 == 0:
            xp = _dense_ffn_layer(xp, ffn_w_gate, ffn_w_up, ffn_w_down, j, fg, fb, alpha=alpha)
            xs = _dense_ffn_layer(xs, ffn_w_gate, ffn_w_up, ffn_w_down, j, fg, fb, alpha=alpha)
        else:
            wr = jnp.pad(moe_w_router[j], ((0, 0), (0, V7X_LANES - ne)))
            wr_hi = wr.astype(BF16)
            wr_lo = (wr - wr_hi.astype(F32)).astype(BF16)
            xp = _moe_layer(xp, wr_hi, wr_lo, moe_w_gate, moe_w_up, moe_w_down, j, fg, fb, ne=ne, alpha=alpha)
            xs = _moe_layer(xs, wr_hi, wr_lo, moe_w_gate, moe_w_up, moe_w_down, j, fg, fb, ne=ne, alpha=alpha)

    return (xp.reshape(bp, sp, d), xs.reshape(bs, ss, d),
            jnp.stack(ckv_p), jnp.stack(kr_p), jnp.stack(pool_p),
            jnp.stack(ckv_s), jnp.stack(kr_s), jnp.stack(pool_s))
```

```python
import functools
import math

import jax
import jax.numpy as jnp
from jax import lax
from jax.experimental import pallas as pl
from jax.experimental.pallas import tpu as pltpu

BF16 = jnp.bfloat16
F32 = jnp.float32

CHUNK = 64
POOL_WINDOWS = (2, 4, 8, 16)
POOL_HALO = 16
ROPE_BASE = 10000.0
LN_EPS = 1e-5
RMS_EPS = 1e-6
TOP_K = 2

V7X_LANES = 128
V7X_VMEM_BUDGET = 56 * 1024 * 1024


def _cparams(*sem):
    return pltpu.CompilerParams(dimension_semantics=sem, vmem_limit_bytes=V7X_VMEM_BUDGET)


def _tile(n, pref):
    if n <= pref:
        return n
    t = pref
    while n % t:
        t //= 2
    assert t >= 8, (n, pref)
    return t


def _layer_norm(z, g, b):
    mu = jnp.mean(z, axis=-1, keepdims=True)
    zc = z - mu
    var = jnp.mean(zc * zc, axis=-1, keepdims=True)
    return zc * lax.rsqrt(var + LN_EPS) * g + b


def _rms(c, g):
    return c * lax.rsqrt(jnp.mean(c * c, axis=-1, keepdims=True) + RMS_EPS) * g


def _mla_proj_kernel(x_ref, w1_ref, w2_ref, qn_ref, kvn_ref, t1_ref, t2_ref,
                     q_ref, ckv_ref, kr_ref, ckr_ref, *, ql, kl, rope, heads, hd):
    xb = x_ref[...].astype(BF16)
    r = jnp.dot(xb, w1_ref[...], preferred_element_type=F32)
    cq = _rms(r[:, :ql], qn_ref[...])
    ckv = _rms(r[:, ql:ql + kl], kvn_ref[...])
    t = r[:, ql + kl:] * t1_ref[...]
    kk = t + pltpu.roll(t, rope, axis=1)
    ckv_ref[...] = ckv
    kr_ref[...] = kk[:, :rope]
    ckr_ref[:, :kl] = ckv.astype(BF16)
    ckr_ref[:, kl:] = kk.astype(BF16)
    cqb = cq.astype(BF16)
    t2 = t2_ref[...]
    for h in range(heads):
        qh = jnp.dot(cqb, w2_ref[:, h * hd:(h + 1) * hd], preferred_element_type=F32)
        q_ref[h] = (qh * t2).astype(BF16)


def _mla_proj(x, w1, w2, qn, kvn, t1, t2, *, ql, kl, rope, heads, hd):
    m, d = x.shape
    bm = _tile(m, 256)
    kern = functools.partial(_mla_proj_kernel, ql=ql, kl=kl, rope=rope, heads=heads, hd=hd)
    row = lambda i: (i, 0)
    full = lambda i: (0, 0)
    return pl.pallas_call(
        kern,
        grid=(m // bm,),
        in_specs=[
            pl.BlockSpec((bm, d), row),
            pl.BlockSpec(w1.shape, full),
            pl.BlockSpec(w2.shape, full),
            pl.BlockSpec((1, ql), full),
            pl.BlockSpec((1, kl), full),
            pl.BlockSpec((bm, 2 * rope), row),
            pl.BlockSpec((bm, hd), row),
        ],
        out_specs=[
            pl.BlockSpec((heads, bm, hd), lambda i: (0, i, 0)),
            pl.BlockSpec((bm, kl), row),
            pl.BlockSpec((bm, rope), row),
            pl.BlockSpec((bm, kl + 2 * rope), row),
        ],
        out_shape=[
            jax.ShapeDtypeStruct((heads, m, hd), BF16),
            jax.ShapeDtypeStruct((m, kl), F32),
            jax.ShapeDtypeStruct((m, rope), F32),
            jax.ShapeDtypeStruct((m, kl + 2 * rope), BF16),
        ],
        compiler_params=_cparams("parallel"),
        name="mla_proj",
    )(x, w1, w2, qn, kvn, t1, t2)


def _kv_up_kernel(ckr_ref, wk_ref, wv_ref, k_ref, v_ref, *, kl, heads, nope, vh):
    c = ckr_ref[:, :kl]
    kk = ckr_ref[:, kl:]
    for p in range(heads // 2):
        kn = jnp.dot(c, wk_ref[:, 2 * p * nope:(2 * p + 2) * nope], preferred_element_type=F32)
        vv = jnp.dot(c, wv_ref[:, 2 * p * vh:(2 * p + 2) * vh], preferred_element_type=F32)
        for s in range(2):
            h = 2 * p + s
            k_ref[h, :, :nope] = kn[:, s * nope:(s + 1) * nope].astype(BF16)
            k_ref[h, :, nope:] = kk
            v_ref[h] = vv[:, s * vh:(s + 1) * vh].astype(BF16)


def _kv_up(ckr, wk, wv, *, kl, heads, nope, vh, hd):
    m = ckr.shape[0]
    bm = _tile(m, 512)
    kern = functools.partial(_kv_up_kernel, kl=kl, heads=heads, nope=nope, vh=vh)
    return pl.pallas_call(
        kern,
        grid=(m // bm,),
        in_specs=[
            pl.BlockSpec((bm, ckr.shape[1]), lambda i: (i, 0)),
            pl.BlockSpec(wk.shape, lambda i: (0, 0)),
            pl.BlockSpec(wv.shape, lambda i: (0, 0)),
        ],
        out_specs=[
            pl.BlockSpec((heads, bm, hd), lambda i: (0, i, 0)),
            pl.BlockSpec((heads, bm, vh), lambda i: (0, i, 0)),
        ],
        out_shape=[
            jax.ShapeDtypeStruct((heads, m, hd), BF16),
            jax.ShapeDtypeStruct((heads, m, vh), BF16),
        ],
        compiler_params=_cparams("parallel"),
        name="kv_up",
    )(ckr, wk, wv)


def _attn_prompt_kernel(q_ref, k_ref, v_ref, o_ref, m_scr, l_scr, acc_scr, *, blk, hp, qs, vh):
    i = pl.program_id(1)
    m_scr[...] = jnp.full(m_scr.shape, -jnp.inf, F32)
    l_scr[...] = jnp.zeros(l_scr.shape, F32)
    acc_scr[...] = jnp.zeros(acc_scr.shape, F32)

    def update(a, u, k, v, diagonal):
        rows = slice(u * blk, (u + 1) * blk)
        s = lax.dot_general(q_ref[a, rows, :], k, (((1,), (1,)), ((), ())), preferred_element_type=F32)
        if diagonal:
            qc = lax.broadcasted_iota(jnp.int32, s.shape, 0) // CHUNK
            kc = lax.broadcasted_iota(jnp.int32, s.shape, 1) // CHUNK
            s = jnp.where(kc <= qc, s, -jnp.inf)
        cols = [s[:, c * V7X_LANES:(c + 1) * V7X_LANES] for c in range(s.shape[1] // V7X_LANES)]
        mx = cols[0]
        for c in cols[1:]:
            mx = jnp.maximum(mx, c)
        m_prev = m_scr[a, rows, :]
        m_new = jnp.maximum(m_prev, jnp.max(mx, axis=-1, keepdims=True))
        alpha = jnp.exp2(m_prev - m_new)
        ps = [jnp.exp2(c - m_new) for c in cols]
        lsum = ps[0]
        for pc in ps[1:]:
            lsum = lsum + pc
        l_scr[a, rows, :] = alpha * l_scr[a, rows, :] + lsum
        p = jnp.concatenate([pc.astype(BF16) for pc in ps], axis=1)
        acc_scr[a, rows, :] = alpha * acc_scr[a, rows, :] + jnp.dot(p, v, preferred_element_type=F32)
        m_scr[a, rows, :] = m_new

    def key_block(start, width, first_sub, diagonal_sub):
        for a in range(hp):
            k = k_ref[a, pl.ds(start, width), :]
            v = v_ref[a, pl.ds(start, width), :]
            for u in range(first_sub, qs):
                update(a, u, k, v, u == diagonal_sub)

    def fully_visible(jj, carry):
        key_block(pl.multiple_of(jj * (blk * qs), blk * qs), blk * qs, 0, -1)
        return carry

    lax.fori_loop(0, i, fully_visible, 0)
    for t in range(qs):
        key_block(pl.multiple_of((i * qs + t) * blk, blk), blk, t, t)
    for a in range(hp):
        l = jnp.sum(l_scr[a], axis=-1, keepdims=True)
        o_ref[:, a * vh:(a + 1) * vh] = (acc_scr[a] / l).astype(o_ref.dtype)


def _attn_prompt(q, k, v, *, blk, hp=2, qs=4):
    heads, s, hd = q.shape
    vh = v.shape[2]
    bq = blk * qs
    assert heads % hp == 0 and vh == V7X_LANES and s % bq == 0
    kern = functools.partial(_attn_prompt_kernel, blk=blk, hp=hp, qs=qs, vh=vh)
    once = pl.Buffered(1)
    return pl.pallas_call(
        kern,
        grid=(heads // hp, s // bq),
        in_specs=[
            pl.BlockSpec((hp, bq, hd), lambda h, i: (h, i, 0)),
            pl.BlockSpec((hp, s, hd), lambda h, i: (h, 0, 0), pipeline_mode=once),
            pl.BlockSpec((hp, s, vh), lambda h, i: (h, 0, 0), pipeline_mode=once),
        ],
        out_specs=pl.BlockSpec((bq, hp * vh), lambda h, i: (i, h)),
        out_shape=jax.ShapeDtypeStruct((s, heads * vh), BF16),
        scratch_shapes=[
            pltpu.VMEM((hp, bq, V7X_LANES), F32),
            pltpu.VMEM((hp, bq, V7X_LANES), F32),
            pltpu.VMEM((hp, bq, vh), F32),
        ],
        compiler_params=_cparams("parallel", "arbitrary"),
        name="attn_prompt",
    )(q, k, v)


def _attn_sample_kernel(q_ref, kc_ref, vc_ref, kn_ref, vn_ref, o_ref, *, past):
    q = q_ref[...]
    dn = (((1,), (1,)), ((), ()))
    s1 = lax.dot_general(q, kc_ref[...], dn, preferred_element_type=F32)
    s2 = lax.dot_general(q, kn_ref[...], dn, preferred_element_type=F32)
    qc1 = (past + lax.broadcasted_iota(jnp.int32, s1.shape, 0)) // CHUNK
    kc1 = lax.broadcasted_iota(jnp.int32, s1.shape, 1) // CHUNK
    s1 = jnp.where(kc1 <= qc1, s1, -jnp.inf)
    qc2 = (past + lax.broadcasted_iota(jnp.int32, s2.shape, 0)) // CHUNK
    kc2 = (past + lax.broadcasted_iota(jnp.int32, s2.shape, 1)) // CHUNK
    s2 = jnp.where(kc2 <= qc2, s2, -jnp.inf)
    m = jnp.maximum(jnp.max(s1, axis=-1, keepdims=True), jnp.max(s2, axis=-1, keepdims=True))
    p1 = jnp.exp2(s1 - m)
    p2 = jnp.exp2(s2 - m)
    l = jnp.sum(p1, axis=-1, keepdims=True) + jnp.sum(p2, axis=-1, keepdims=True)
    o = (jnp.dot(p1.astype(BF16), vc_ref[...], preferred_element_type=F32)
         + jnp.dot(p2.astype(BF16), vn_ref[...], preferred_element_type=F32))
    o_ref[...] = (o / l).astype(o_ref.dtype)


def _attn_sample(q, kc, vc, kn, vn, *, batch, past):
    heads, m, hd = q.shape
    sq = m // batch
    vh = vc.shape[2]
    kern = functools.partial(_attn_sample_kernel, past=past)
    return pl.pallas_call(
        kern,
        grid=(batch, heads),
        in_specs=[
            pl.BlockSpec((None, sq, hd), lambda b, h: (h, b, 0)),
            pl.BlockSpec((None, past, hd), lambda b, h: (h, b, 0)),
            pl.BlockSpec((None, past, vh), lambda b, h: (h, b, 0)),
            pl.BlockSpec((None, sq, hd), lambda b, h: (h, b, 0)),
            pl.BlockSpec((None, sq, vh), lambda b, h: (h, b, 0)),
        ],
        out_specs=pl.BlockSpec((sq, vh), lambda b, h: (b, h)),
        out_shape=jax.ShapeDtypeStruct((m, heads * vh), BF16),
        compiler_params=_cparams("parallel", "parallel"),
        name="attn_sample",
    )(q, kc, vc, kn, vn)


def _proj_res_ln_kernel(a_ref, w_ref, x_ref, g_ref, b_ref, o_ref, *, alpha):
    y = jnp.dot(a_ref[...], w_ref[...], preferred_element_type=F32)
    o_ref[...] = _layer_norm(alpha * x_ref[...] + y, g_ref[...], b_ref[...])


def _proj_res_ln(a, w, x, g, b, *, alpha):
    m, k = a.shape
    d = w.shape[1]
    bm = _tile(m, 512)
    row = lambda i: (i, 0)
    full = lambda i: (0, 0)
    return pl.pallas_call(
        functools.partial(_proj_res_ln_kernel, alpha=alpha),
        grid=(m // bm,),
        in_specs=[
            pl.BlockSpec((bm, k), row),
            pl.BlockSpec((k, d), full),
            pl.BlockSpec((bm, d), row),
            pl.BlockSpec((1, d), full),
            pl.BlockSpec((1, d), full),
        ],
        out_specs=pl.BlockSpec((bm, d), row),
        out_shape=jax.ShapeDtypeStruct((m, d), F32),
        compiler_params=_cparams("parallel"),
        name="proj_res_ln",
    )(a, w, x, g, b)


def _pool_kernel(x_ref, prev_ref, hist_ref, w_ref, sc_ref, g_ref, b_ref, o_ref, buf, y_scr,
                 *, bm, gw, pos0, alpha):
    i = pl.program_id(1)
    x = x_ref[...]
    halo = jnp.where(i == 0, hist_ref[...], prev_ref[...])
    buf[0:POOL_HALO, :] = halo
    buf[POOL_HALO:POOL_HALO + bm, :] = x
    pos = pos0 + i * bm + lax.broadcasted_iota(jnp.int32, (bm, 1), 0)
    for gi, w in enumerate(POOL_WINDOWS):
        lo, hi = gi * gw, (gi + 1) * gw
        tot = x[:, lo:hi]
        for j in range(1, w):
            tot = tot + buf[POOL_HALO - j:POOL_HALO - j + bm, lo:hi]
        cnt = jnp.minimum(pos + 1, w).astype(F32)
        dlt = (tot / cnt - x[:, lo:hi]).astype(BF16)
        y_scr[:, lo:hi] = jnp.dot(dlt, w_ref[gi], preferred_element_type=F32)
    y = y_scr[...] * sc_ref[...]
    o_ref[...] = _layer_norm(alpha * x + y, g_ref[...], b_ref[...])


def _pool_layer(x, hist, w, sc, g, b, *, pos0, alpha):
    bsz, s, d = x.shape
    gw = d // len(POOL_WINDOWS)
    bm = _tile(s, 512)
    assert bm % POOL_HALO == 0
    hb = bm // POOL_HALO
    kern = functools.partial(_pool_kernel, bm=bm, gw=gw, pos0=pos0, alpha=alpha)
    vec = lambda bb, i: (0, 0)
    return pl.pallas_call(
        kern,
        grid=(bsz, s // bm),
        in_specs=[
            pl.BlockSpec((None, bm, d), lambda bb, i: (bb, i, 0)),
            pl.BlockSpec((None, POOL_HALO, d), lambda bb, i: (bb, jnp.maximum(i * hb - 1, 0), 0)),
            pl.BlockSpec((None, POOL_HALO, d), lambda bb, i: (bb, 0, 0)),
            pl.BlockSpec(w.shape, lambda bb, i: (0, 0, 0)),
            pl.BlockSpec((1, d), vec),
            pl.BlockSpec((1, d), vec),
            pl.BlockSpec((1, d), vec),
        ],
        out_specs=pl.BlockSpec((None, bm, d), lambda bb, i: (bb, i, 0)),
        out_shape=jax.ShapeDtypeStruct((bsz, s, d), F32),
        scratch_shapes=[
            pltpu.VMEM((bm + POOL_HALO, d), F32),
            pltpu.VMEM((bm, d), F32),
        ],
        compiler_params=_cparams("parallel", "arbitrary"),
        name="pool_mix",
    )(x, x, hist, w, sc, g, b)


def _swiglu_tile(tv_ref, x_ref, wg_ref, wu_ref, wd_ref, o_ref, xb_scr, wgb, wub, wdb, *, bm, sb):
    i = pl.program_id(0)
    f = pl.program_id(1)
    valid = tv_ref[i]

    @pl.when(f == 0)
    def _():
        o_ref[...] = jnp.zeros(o_ref.shape, F32)
        xb_scr[...] = x_ref[...].astype(BF16)

    def rows(lo, hi, wg, wu, wd):
        xs = xb_scr[lo:hi, :]
        gte = jnp.dot(xs, wg, preferred_element_type=F32)
        up = jnp.dot(xs, wu, preferred_element_type=F32)
        hid = (gte * jax.nn.sigmoid(gte) * up).astype(BF16)
        o_ref[lo:hi, :] += jnp.dot(hid, wd, preferred_element_type=F32)

    @pl.when(valid == bm)
    def _():
        rows(0, bm, wg_ref[...].astype(BF16), wu_ref[...].astype(BF16), wd_ref[...].astype(BF16))

    @pl.when((valid > 0) & (valid < bm))
    def _():
        wgb[...] = wg_ref[...].astype(BF16)
        wub[...] = wu_ref[...].astype(BF16)
        wdb[...] = wd_ref[...].astype(BF16)

    for s in range(bm // sb):
        @pl.when((s * sb < valid) & (valid < bm))
        def _():
            rows(s * sb, (s + 1) * sb, wgb[...], wub[...], wdb[...])


def _ffn_dense_kernel(te_ref, tv_ref, x_ref, wg_ref, wu_ref, wd_ref, g_ref, b_ref, o_ref,
                      xb_scr, wgb, wub, wdb, *, bm, sb, alpha):
    del te_ref
    _swiglu_tile(tv_ref, x_ref, wg_ref, wu_ref, wd_ref, o_ref, xb_scr, wgb, wub, wdb, bm=bm, sb=sb)

    @pl.when(pl.program_id(1) == pl.num_programs(1) - 1)
    def _():
        o_ref[...] = _layer_norm(alpha * x_ref[...] + o_ref[...], g_ref[...], b_ref[...])


def _ffn_expert_kernel(te_ref, tv_ref, x_ref, wg_ref, wu_ref, wd_ref, o_ref,
                       xb_scr, wgb, wub, wdb, *, bm, sb):
    del te_ref
    _swiglu_tile(tv_ref, x_ref, wg_ref, wu_ref, wd_ref, o_ref, xb_scr, wgb, wub, wdb, bm=bm, sb=sb)


def _ffn_call(x, wg, wu, wd, layer, tile_expert, tile_valid, *, bm, ln=None, alpha=None):
    r, d = x.shape
    nf_total = wg.shape[3]
    bf = _tile(nf_total, 256)
    nf = nf_total // bf
    sb = _tile(bm, 256)
    nt = r // bm

    def chunk(i, f, te, tv):
        return jnp.where(tv[i] > 0, f, nf - 1)

    in_specs = [
        pl.BlockSpec((bm, d), lambda i, f, te, tv: (i, 0), pipeline_mode=pl.Buffered(1)),
        pl.BlockSpec((None, None, d, bf), lambda i, f, te, tv: (layer, te[i], 0, chunk(i, f, te, tv))),
        pl.BlockSpec((None, None, d, bf), lambda i, f, te, tv: (layer, te[i], 0, chunk(i, f, te, tv))),
        pl.BlockSpec((None, None, bf, d), lambda i, f, te, tv: (layer, te[i], chunk(i, f, te, tv), 0)),
    ]
    args = [x, wg, wu, wd]
    if ln is not None:
        in_specs += [pl.BlockSpec((1, d), lambda i, f, te, tv: (0, 0))] * 2
        args += list(ln)
        kern = functools.partial(_ffn_dense_kernel, bm=bm, sb=sb, alpha=alpha)
    else:
        kern = functools.partial(_ffn_expert_kernel, bm=bm, sb=sb)
    return pl.pallas_call(
        kern,
        grid_spec=pltpu.PrefetchScalarGridSpec(
            num_scalar_prefetch=2,
            grid=(nt, nf),
            in_specs=in_specs,
            out_specs=pl.BlockSpec((bm, d), lambda i, f, te, tv: (i, 0)),
            scratch_shapes=[pltpu.VMEM((bm, d), BF16), pltpu.VMEM((d, bf), BF16),
                            pltpu.VMEM((d, bf), BF16), pltpu.VMEM((bf, d), BF16)],
        ),
        out_shape=jax.ShapeDtypeStruct((r, d), F32),
        compiler_params=_cparams("parallel", "arbitrary"),
        name="swiglu_dense" if ln is not None else "swiglu_experts",
    )(tile_expert, tile_valid, *args)


def _router_kernel(x_ref, wh_ref, wl_ref, o_ref, *, ne):
    x = x_ref[...]
    xh = x.astype(BF16)
    xl = (x - xh.astype(F32)).astype(BF16)
    wh = wh_ref[...]
    lg = (jnp.dot(xh, wh, preferred_element_type=F32)
          + (jnp.dot(xl, wh, preferred_element_type=F32)
             + jnp.dot(xh, wl_ref[...], preferred_element_type=F32)))
    lane = lax.broadcasted_iota(jnp.int32, lg.shape, 1)
    lg = jnp.where(lane < ne, lg, -jnp.inf)
    m1 = jnp.max(lg, axis=-1, keepdims=True)
    i1 = jnp.min(jnp.where(lg == m1, lane, V7X_LANES), axis=-1, keepdims=True)
    lg2 = jnp.where(lane == i1, -jnp.inf, lg)
    m2 = jnp.max(lg2, axis=-1, keepdims=True)
    i2 = jnp.min(jnp.where(lg2 == m2, lane, V7X_LANES), axis=-1, keepdims=True)
    e = jnp.exp(m2 - m1)
    den = 1.0 + e
    g1 = 1.0 / den
    g2 = e / den
    out = jnp.where(lane == 0, i1.astype(F32),
                    jnp.where(lane == 1, i2.astype(F32),
                              jnp.where(lane == 2, g1, jnp.where(lane == 3, g2, 0.0))))
    o_ref[...] = out


def _router(x, wh, wl, *, ne):
    m, d = x.shape
    bm = _tile(m, 512)
    return pl.pallas_call(
        functools.partial(_router_kernel, ne=ne),
        grid=(m // bm,),
        in_specs=[
            pl.BlockSpec((bm, d), lambda i: (i, 0)),
            pl.BlockSpec(wh.shape, lambda i: (0, 0)),
            pl.BlockSpec(wl.shape, lambda i: (0, 0)),
        ],
        out_specs=pl.BlockSpec((bm, V7X_LANES), lambda i: (i, 0)),
        out_shape=jax.ShapeDtypeStruct((m, V7X_LANES), F32),
        compiler_params=_cparams("parallel"),
        name="router",
    )(x, wh, wl)


DMA_ISSUE_UNROLL = 8


def _row_copy(src, src_row, dst, dst_row, sem):
    return pltpu.make_async_copy(src.at[pl.ds(src_row, 1)], dst.at[pl.ds(dst_row, 1)], sem)


def _dispatch_kernel(pos_ref, x_ref, init_hbm, xs_hbm, sem, *, bt):
    del init_hbm
    base = pl.program_id(0) * bt

    def issue(r, carry):
        for k in range(TOP_K):
            _row_copy(x_ref, r, xs_hbm, pos_ref[(base + r) * TOP_K + k], sem).start()
        return carry

    def drain(r, carry):
        for k in range(TOP_K):
            _row_copy(x_ref, 0, xs_hbm, 0, sem).wait()
        return carry

    lax.fori_loop(0, bt, issue, 0, unroll=DMA_ISSUE_UNROLL)
    lax.fori_loop(0, bt, drain, 0, unroll=DMA_ISSUE_UNROLL)


def _dispatch(x, pos, init):
    m, d = x.shape
    rows = init.shape[0]
    bt = _tile(m, 256)
    return pl.pallas_call(
        functools.partial(_dispatch_kernel, bt=bt),
        grid_spec=pltpu.PrefetchScalarGridSpec(
            num_scalar_prefetch=1,
            grid=(m // bt,),
            in_specs=[pl.BlockSpec((bt, d), lambda i, p: (i, 0)), pl.BlockSpec(memory_space=pl.ANY)],
            out_specs=pl.BlockSpec(memory_space=pl.ANY),
            scratch_shapes=[pltpu.SemaphoreType.DMA(())],
        ),
        out_shape=jax.ShapeDtypeStruct((rows, d), x.dtype),
        input_output_aliases={2: 0},
        compiler_params=_cparams("arbitrary"),
        name="moe_dispatch",
    )(pos, x, init)


def _combine_kernel(pos_ref, ys_hbm, x_ref, rt_ref, g_ref, b_ref, o_ref, buf, sem, *, bt, alpha):
    base = pl.program_id(0) * bt

    def issue(r, carry):
        for k in range(TOP_K):
            _row_copy(ys_hbm, pos_ref[(base + r) * TOP_K + k], buf.at[k], r, sem).start()
        return carry

    def drain(r, carry):
        for k in range(TOP_K):
            _row_copy(ys_hbm, 0, buf.at[k], 0, sem).wait()
        return carry

    lax.fori_loop(0, bt, issue, 0, unroll=DMA_ISSUE_UNROLL)
    lax.fori_loop(0, bt, drain, 0, unroll=DMA_ISSUE_UNROLL)
    rt = rt_ref[...]
    y = buf[0] * rt[:, TOP_K:TOP_K + 1]
    for k in range(1, TOP_K):
        y = y + buf[k] * rt[:, TOP_K + k:TOP_K + k + 1]
    o_ref[...] = _layer_norm(alpha * x_ref[...] + y, g_ref[...], b_ref[...])


def _combine(ys, pos, x, route, g, b, *, alpha):
    m, d = x.shape
    bt = _tile(m, 256)
    row = lambda i, p: (i, 0)
    full = lambda i, p: (0, 0)
    return pl.pallas_call(
        functools.partial(_combine_kernel, bt=bt, alpha=alpha),
        grid_spec=pltpu.PrefetchScalarGridSpec(
            num_scalar_prefetch=1,
            grid=(m // bt,),
            in_specs=[
                pl.BlockSpec(memory_space=pl.ANY),
                pl.BlockSpec((bt, d), row),
                pl.BlockSpec((bt, V7X_LANES), row),
                pl.BlockSpec((1, d), full),
                pl.BlockSpec((1, d), full),
            ],
            out_specs=pl.BlockSpec((bt, d), row),
            scratch_shapes=[pltpu.VMEM((TOP_K, bt, d), F32), pltpu.SemaphoreType.DMA(())],
        ),
        out_shape=jax.ShapeDtypeStruct((m, d), F32),
        compiler_params=_cparams("arbitrary"),
        name="moe_combine",
    )(pos, ys, x, route, g, b)


def _moe_plan(idx, ne, bm, nt):
    e_flat = idx.reshape(-1)
    onehot = (e_flat[:, None] == jnp.arange(ne, dtype=jnp.int32)[None, :]).astype(jnp.int32)
    csum = jnp.cumsum(onehot, axis=0)
    rank = jnp.take_along_axis(csum, e_flat[:, None], axis=1)[:, 0] - 1
    counts = csum[-1]
    tiles_e = (counts + bm - 1) // bm
    tile_end = jnp.cumsum(tiles_e)
    tile_start = tile_end - tiles_e
    pos = (tile_start[e_flat] * bm + rank).astype(jnp.int32)
    t = jnp.arange(nt, dtype=jnp.int32)
    te = jnp.minimum(jnp.sum((t[:, None] >= tile_end[None, :]).astype(jnp.int32), axis=1), ne - 1)
    tv = jnp.clip(counts[te] - (t - tile_start[te]) * bm, 0, bm).astype(jnp.int32)
    return pos, te, tv


FFN_ROW_TILE = 1024


def _moe_layer(xs, wr_hi, wr_lo, wg, wu, wd, layer, g, b, *, ne, alpha):
    d = xs[0].shape[1]
    routes = [_router(x, wr_hi, wr_lo, ne=ne) for x in xs]
    idx = jnp.concatenate([r[:, :TOP_K] for r in routes], axis=0).astype(jnp.int32)
    bm = FFN_ROW_TILE
    nt = (idx.shape[0] * TOP_K) // bm + ne
    pos, te, tv = _moe_plan(idx, ne, bm, nt)
    bounds = [0]
    for x in xs:
        bounds.append(bounds[-1] + x.shape[0] * TOP_K)
    slots = [pos[lo:hi] for lo, hi in zip(bounds[:-1], bounds[1:])]
    sorted_rows = jnp.zeros((nt * bm, d), F32)
    for x, p in zip(xs, slots):
        sorted_rows = _dispatch(x, p, sorted_rows)
    ys = _ffn_call(sorted_rows, wg, wu, wd, layer, te, tv, bm=bm)
    return [_combine(ys, p, x, r, g, b, alpha=alpha) for x, p, r in zip(xs, slots, routes)]


def _dense_ffn_layer(x, wg, wu, wd, layer, g, b, *, alpha):
    m = x.shape[0]
    bm = _tile(m, FFN_ROW_TILE)
    nt = m // bm
    te = jnp.zeros((nt,), jnp.int32)
    tv = jnp.full((nt,), bm, jnp.int32)
    return _ffn_call(x, wg[:, None], wu[:, None], wd[:, None], layer, te, tv,
                     bm=bm, ln=(g, b), alpha=alpha)


def _rope_tables(pos, rope, scale):
    inv = ROPE_BASE ** (-jnp.arange(0, rope, 2, dtype=F32) / rope)
    ang = pos.astype(F32)[:, None] * inv[None, :]
    cos, sin = jnp.cos(ang), jnp.sin(ang)
    t1 = jnp.concatenate([cos, cos, sin, sin], axis=1)
    ones = jnp.ones((pos.shape[0], V7X_LANES), F32)
    t2 = jnp.concatenate([ones, t1], axis=1) * scale
    return t1, t2


def _swap_halves(w):
    half = w.shape[-1] // 2
    return jnp.concatenate([-w[..., half:], w[..., :half]], axis=-1)


def _prep_mla(w_dq, w_uq, w_dkv, w_uk, w_uv, w_o, *, kl, heads, nope, rope):
    ql = w_dq.shape[1]
    kr = w_dkv[:, kl:]
    w1 = jnp.concatenate([w_dq, w_dkv[:, :kl], kr, _swap_halves(kr)], axis=1).astype(BF16)
    uq = w_uq.reshape(ql, heads, nope + rope)
    qr = uq[..., nope:]
    w2 = jnp.concatenate([uq[..., :nope], qr, _swap_halves(qr)], axis=-1)
    w2 = w2.reshape(ql, heads * (nope + 2 * rope)).astype(BF16)
    wk = w_uk.reshape(kl, -1).astype(BF16)
    wv = w_uv.reshape(kl, -1).astype(BF16)
    return w1, w2, wk, wv, w_o.astype(BF16)


def kernel(x_prompt, x_sample, cache_ckv, cache_krope, state_pool, mla_w_dq, mla_q_norm, mla_w_uq, mla_w_dkv, mla_kv_norm, mla_w_uk, mla_w_uv, mla_w_o, pool_w, pool_scale, ffn_w_gate, ffn_w_up, ffn_w_down, moe_w_router, moe_w_gate, moe_w_up, moe_w_down, ln_mix_g, ln_mix_b, ln_ffn_g, ln_ffn_b):
    bp, sp, d = x_prompt.shape
    bs, ss, _ = x_sample.shape
    past = cache_ckv.shape[2]
    depth = ln_mix_g.shape[0]
    kl, heads, nope = mla_w_uk.shape[1:]
    vh = mla_w_uv.shape[3]
    ql = mla_w_dq.shape[2]
    rope = mla_w_dkv.shape[2] - kl
    hd = nope + 2 * rope
    ne = moe_w_router.shape[2]
    alpha = (2.0 * depth) ** 0.25
    scale = math.log2(math.e) / math.sqrt(nope + rope)
    assert bp == 1 and heads % 2 == 0 and 2 * rope == V7X_LANES and nope == V7X_LANES
    assert past % CHUNK == 0 and state_pool.shape[2] == POOL_HALO - 1

    t1_p, t2_p = _rope_tables(jnp.arange(sp, dtype=jnp.int32), rope, scale)
    t1_s, t2_s = _rope_tables(jnp.tile(past + jnp.arange(ss, dtype=jnp.int32), bs), rope, scale)

    xp = x_prompt.reshape(sp, d)
    xs = x_sample.reshape(bs * ss, d)
    row = lambda v: v.reshape(1, -1)
    ckv_p, kr_p, pool_p, ckv_s, kr_s, pool_s = [], [], [], [], [], []
    blk = _tile(sp, 512)

    for i in range(depth):
        j = i // 2
        if i % 2 == 0:
            w1, w2, wk, wv, wo = _prep_mla(mla_w_dq[j], mla_w_uq[j], mla_w_dkv[j], mla_w_uk[j],
                                           mla_w_uv[j], mla_w_o[j], kl=kl, heads=heads, nope=nope, rope=rope)
            dims = dict(ql=ql, kl=kl, rope=rope, heads=heads, hd=hd)
            up = dict(kl=kl, heads=heads, nope=nope, vh=vh, hd=hd)
            qn, kvn = row(mla_q_norm[j]), row(mla_kv_norm[j])
            lg, lb = row(ln_mix_g[i]), row(ln_mix_b[i])
            q, ckv, kr, ckr = _mla_proj(xp, w1, w2, qn, kvn, t1_p, t2_p, **dims)
            k, v = _kv_up(ckr, wk, wv, **up)
            o = _attn_prompt(q, k, v, blk=blk)
            xp = _proj_res_ln(o, wo, xp, lg, lb, alpha=alpha)
            ckv_p.append(ckv.reshape(bp, sp, kl))
            kr_p.append(kr.reshape(bp, sp, rope))
            q, ckv, kr, ckr = _mla_proj(xs, w1, w2, qn, kvn, t1_s, t2_s, **dims)
            kn, vn = _kv_up(ckr, wk, wv, **up)
            ckr_c = jnp.concatenate([cache_ckv[j], cache_krope[j], cache_krope[j]], axis=-1)
            kc, vc = _kv_up(ckr_c.reshape(bs * past, kl + 2 * rope).astype(BF16), wk, wv, **up)
            o = _attn_sample(q, kc, vc, kn, vn, batch=bs, past=past)
            xs = _proj_res_ln(o, wo, xs, lg, lb, alpha=alpha)
            ckv_s.append(ckv.reshape(bs, ss, kl))
            kr_s.append(kr.reshape(bs, ss, rope))
        else:
            pw = pool_w[j].astype(BF16)
            sc = row(pool_scale[j])
            lg, lb = row(ln_mix_g[i]), row(ln_mix_b[i])
            xp3 = xp.reshape(bp, sp, d)
            xs3 = xs.reshape(bs, ss, d)
            pool_p.append(xp3[:, sp - (POOL_HALO - 1):, :])
            pool_s.append(jnp.concatenate([state_pool[j], xs3], axis=1)[:, ss:, :])
            hist_p = jnp.zeros((bp, POOL_HALO, d), F32)
            hist_s = jnp.concatenate([jnp.zeros((bs, 1, d), F32), state_pool[j]], axis=1)
            xp = _pool_layer(xp3, hist_p, pw, sc, lg, lb, pos0=0, alpha=alpha).reshape(sp, d)
            xs = _pool_layer(xs3, hist_s, pw, sc, lg, lb, pos0=past, alpha=alpha).reshape(bs * ss, d)

        fg, fb = row(ln_ffn_g[i]), row(ln_ffn_b[i])
        if i % 2 == 0:
            xp = _dense_ffn_layer(xp, ffn_w_gate, ffn_w_up, ffn_w_down, j, fg, fb, alpha=alpha)
            xs = _dense_ffn_layer(xs, ffn_w_gate, ffn_w_up, ffn_w_down, j, fg, fb, alpha=alpha)
        else:
            wr = jnp.pad(moe_w_router[j], ((0, 0), (0, V7X_LANES - ne)))
            wr_hi = wr.astype(BF16)
            wr_lo = (wr - wr_hi.astype(F32)).astype(BF16)
            xp, xs = _moe_layer([xp, xs], wr_hi, wr_lo, moe_w_gate, moe_w_up, moe_w_down, j, fg, fb,
                                ne=ne, alpha=alpha)

    return (xp.reshape(bp, sp, d), xs.reshape(bs, ss, d),
            jnp.stack(ckv_p), jnp.stack(kr_p), jnp.stack(pool_p),
            jnp.stack(ckv_s), jnp.stack(kr_s), jnp.stack(pool_s))
```

```python
import functools
import math

import jax
import jax.numpy as jnp
from jax import lax
from jax.experimental import pallas as pl
from jax.experimental.pallas import tpu as pltpu

BF16 = jnp.bfloat16
F32 = jnp.float32

CHUNK = 64
POOL_WINDOWS = (2, 4, 8, 16)
POOL_HALO = 16
ROPE_BASE = 10000.0
LN_EPS = 1e-5
RMS_EPS = 1e-6
TOP_K = 2

V7X_LANES = 128
V7X_VMEM_BUDGET = 56 * 1024 * 1024


def _cparams(*sem):
    return pltpu.CompilerParams(dimension_semantics=sem, vmem_limit_bytes=V7X_VMEM_BUDGET)


def _tile(n, pref):
    if n <= pref:
        return n
    t = pref
    while n % t:
        t //= 2
    assert t >= 8, (n, pref)
    return t


def _layer_norm(z, g, b):
    mu = jnp.mean(z, axis=-1, keepdims=True)
    zc = z - mu
    var = jnp.mean(zc * zc, axis=-1, keepdims=True)
    return zc * lax.rsqrt(var + LN_EPS) * g + b


def _rms(c, g):
    return c * lax.rsqrt(jnp.mean(c * c, axis=-1, keepdims=True) + RMS_EPS) * g


def _mla_proj_kernel(x_ref, w1_ref, w2_ref, qn_ref, kvn_ref, t1_ref, t2_ref,
                     q_ref, ckv_ref, kr_ref, ckr_ref, *, ql, kl, rope, heads, hd):
    xb = x_ref[...].astype(BF16)
    r = jnp.dot(xb, w1_ref[...], preferred_element_type=F32)
    cq = _rms(r[:, :ql], qn_ref[...])
    ckv = _rms(r[:, ql:ql + kl], kvn_ref[...])
    t = r[:, ql + kl:] * t1_ref[...]
    kk = t + pltpu.roll(t, rope, axis=1)
    ckv_ref[...] = ckv
    kr_ref[...] = kk[:, :rope]
    ckr_ref[:, :kl] = ckv.astype(BF16)
    ckr_ref[:, kl:] = kk.astype(BF16)
    cqb = cq.astype(BF16)
    t2 = t2_ref[...]
    for h in range(heads):
        qh = jnp.dot(cqb, w2_ref[:, h * hd:(h + 1) * hd], preferred_element_type=F32)
        q_ref[h] = (qh * t2).astype(BF16)


def _mla_proj(x, w1, w2, qn, kvn, t1, t2, *, ql, kl, rope, heads, hd):
    m, d = x.shape
    bm = _tile(m, 256)
    kern = functools.partial(_mla_proj_kernel, ql=ql, kl=kl, rope=rope, heads=heads, hd=hd)
    row = lambda i: (i, 0)
    full = lambda i: (0, 0)
    return pl.pallas_call(
        kern,
        grid=(m // bm,),
        in_specs=[
            pl.BlockSpec((bm, d), row),
            pl.BlockSpec(w1.shape, full),
            pl.BlockSpec(w2.shape, full),
            pl.BlockSpec((1, ql), full),
            pl.BlockSpec((1, kl), full),
            pl.BlockSpec((bm, 2 * rope), row),
            pl.BlockSpec((bm, hd), row),
        ],
        out_specs=[
            pl.BlockSpec((heads, bm, hd), lambda i: (0, i, 0)),
            pl.BlockSpec((bm, kl), row),
            pl.BlockSpec((bm, rope), row),
            pl.BlockSpec((bm, kl + 2 * rope), row),
        ],
        out_shape=[
            jax.ShapeDtypeStruct((heads, m, hd), BF16),
            jax.ShapeDtypeStruct((m, kl), F32),
            jax.ShapeDtypeStruct((m, rope), F32),
            jax.ShapeDtypeStruct((m, kl + 2 * rope), BF16),
        ],
        compiler_params=_cparams("parallel"),
        name="mla_proj",
    )(x, w1, w2, qn, kvn, t1, t2)


def _kv_up_kernel(ckr_ref, wk_ref, wv_ref, k_ref, v_ref, *, kl, heads, nope, vh):
    c = ckr_ref[:, :kl]
    kk = ckr_ref[:, kl:]
    for p in range(heads // 2):
        kn = jnp.dot(c, wk_ref[:, 2 * p * nope:(2 * p + 2) * nope], preferred_element_type=F32)
        vv = jnp.dot(c, wv_ref[:, 2 * p * vh:(2 * p + 2) * vh], preferred_element_type=F32)
        for s in range(2):
            h = 2 * p + s
            k_ref[h, :, :nope] = kn[:, s * nope:(s + 1) * nope].astype(BF16)
            k_ref[h, :, nope:] = kk
            v_ref[h] = vv[:, s * vh:(s + 1) * vh].astype(BF16)


def _kv_up(ckr, wk, wv, *, kl, heads, nope, vh, hd):
    m = ckr.shape[0]
    bm = _tile(m, 512)
    kern = functools.partial(_kv_up_kernel, kl=kl, heads=heads, nope=nope, vh=vh)
    return pl.pallas_call(
        kern,
        grid=(m // bm,),
        in_specs=[
            pl.BlockSpec((bm, ckr.shape[1]), lambda i: (i, 0)),
            pl.BlockSpec(wk.shape, lambda i: (0, 0)),
            pl.BlockSpec(wv.shape, lambda i: (0, 0)),
        ],
        out_specs=[
            pl.BlockSpec((heads, bm, hd), lambda i: (0, i, 0)),
            pl.BlockSpec((heads, bm, vh), lambda i: (0, i, 0)),
        ],
        out_shape=[
            jax.ShapeDtypeStruct((heads, m, hd), BF16),
            jax.ShapeDtypeStruct((heads, m, vh), BF16),
        ],
        compiler_params=_cparams("parallel"),
        name="kv_up",
    )(ckr, wk, wv)


def _attn_prompt_kernel(q_ref, k_ref, v_ref, o_ref, m_scr, l_scr, acc_scr, *, blk, hp, qs, vh):
    i = pl.program_id(1)
    m_scr[...] = jnp.full(m_scr.shape, -jnp.inf, F32)
    l_scr[...] = jnp.zeros(l_scr.shape, F32)
    acc_scr[...] = jnp.zeros(acc_scr.shape, F32)

    def update(a, u, k, v, diagonal):
        rows = slice(u * blk, (u + 1) * blk)
        s = lax.dot_general(q_ref[a, rows, :], k, (((1,), (1,)), ((), ())), preferred_element_type=F32)
        if diagonal:
            qc = lax.broadcasted_iota(jnp.int32, s.shape, 0) // CHUNK
            kc = lax.broadcasted_iota(jnp.int32, s.shape, 1) // CHUNK
            s = jnp.where(kc <= qc, s, -jnp.inf)
        cols = [s[:, c * V7X_LANES:(c + 1) * V7X_LANES] for c in range(s.shape[1] // V7X_LANES)]
        mx = cols[0]
        for c in cols[1:]:
            mx = jnp.maximum(mx, c)
        m_prev = m_scr[a, rows, :]
        m_new = jnp.maximum(m_prev, jnp.max(mx, axis=-1, keepdims=True))
        alpha = jnp.exp2(m_prev - m_new)
        ps = [jnp.exp2(c - m_new) for c in cols]
        lsum = ps[0]
        for pc in ps[1:]:
            lsum = lsum + pc
        l_scr[a, rows, :] = alpha * l_scr[a, rows, :] + lsum
        p = jnp.concatenate([pc.astype(BF16) for pc in ps], axis=1)
        acc_scr[a, rows, :] = alpha * acc_scr[a, rows, :] + jnp.dot(p, v, preferred_element_type=F32)
        m_scr[a, rows, :] = m_new

    def key_block(start, width, subs, diagonal):
        for a in range(hp):
            k = k_ref[a, pl.ds(start, width), :]
            v = v_ref[a, pl.ds(start, width), :]
            for u in subs:
                update(a, u, k, v, diagonal)

    def fully_visible(jj, carry):
        key_block(pl.multiple_of(jj * (blk * qs), blk * qs), blk * qs, range(qs), False)
        return carry

    lax.fori_loop(0, i, fully_visible, 0)
    own = pl.multiple_of(i * (blk * qs), blk * qs)
    for u in range(qs):
        if u:
            key_block(own, u * blk, [u], False)
        key_block(pl.multiple_of(own + u * blk, blk), blk, [u], True)
    for a in range(hp):
        l = jnp.sum(l_scr[a], axis=-1, keepdims=True)
        o_ref[:, a * vh:(a + 1) * vh] = (acc_scr[a] / l).astype(o_ref.dtype)


def _attn_prompt(q, k, v, *, blk, hp=2, qs=4):
    heads, s, hd = q.shape
    vh = v.shape[2]
    bq = blk * qs
    assert heads % hp == 0 and vh == V7X_LANES and s % bq == 0
    kern = functools.partial(_attn_prompt_kernel, blk=blk, hp=hp, qs=qs, vh=vh)
    once = pl.Buffered(1)
    return pl.pallas_call(
        kern,
        grid=(heads // hp, s // bq),
        in_specs=[
            pl.BlockSpec((hp, bq, hd), lambda h, i: (h, i, 0)),
            pl.BlockSpec((hp, s, hd), lambda h, i: (h, 0, 0), pipeline_mode=once),
            pl.BlockSpec((hp, s, vh), lambda h, i: (h, 0, 0), pipeline_mode=once),
        ],
        out_specs=pl.BlockSpec((bq, hp * vh), lambda h, i: (i, h)),
        out_shape=jax.ShapeDtypeStruct((s, heads * vh), BF16),
        scratch_shapes=[
            pltpu.VMEM((hp, bq, V7X_LANES), F32),
            pltpu.VMEM((hp, bq, V7X_LANES), F32),
            pltpu.VMEM((hp, bq, vh), F32),
        ],
        compiler_params=_cparams("parallel", "arbitrary"),
        name="attn_prompt",
    )(q, k, v)


def _attn_sample_kernel(q_ref, kc_ref, vc_ref, kn_ref, vn_ref, o_ref, *, past):
    q = q_ref[...]
    dn = (((1,), (1,)), ((), ()))
    s1 = lax.dot_general(q, kc_ref[...], dn, preferred_element_type=F32)
    s2 = lax.dot_general(q, kn_ref[...], dn, preferred_element_type=F32)
    qc1 = (past + lax.broadcasted_iota(jnp.int32, s1.shape, 0)) // CHUNK
    kc1 = lax.broadcasted_iota(jnp.int32, s1.shape, 1) // CHUNK
    s1 = jnp.where(kc1 <= qc1, s1, -jnp.inf)
    qc2 = (past + lax.broadcasted_iota(jnp.int32, s2.shape, 0)) // CHUNK
    kc2 = (past + lax.broadcasted_iota(jnp.int32, s2.shape, 1)) // CHUNK
    s2 = jnp.where(kc2 <= qc2, s2, -jnp.inf)
    m = jnp.maximum(jnp.max(s1, axis=-1, keepdims=True), jnp.max(s2, axis=-1, keepdims=True))
    p1 = jnp.exp2(s1 - m)
    p2 = jnp.exp2(s2 - m)
    l = jnp.sum(p1, axis=-1, keepdims=True) + jnp.sum(p2, axis=-1, keepdims=True)
    o = (jnp.dot(p1.astype(BF16), vc_ref[...], preferred_element_type=F32)
         + jnp.dot(p2.astype(BF16), vn_ref[...], preferred_element_type=F32))
    o_ref[...] = (o / l).astype(o_ref.dtype)


def _attn_sample(q, kc, vc, kn, vn, *, batch, past):
    heads, m, hd = q.shape
    sq = m // batch
    vh = vc.shape[2]
    kern = functools.partial(_attn_sample_kernel, past=past)
    return pl.pallas_call(
        kern,
        grid=(batch, heads),
        in_specs=[
            pl.BlockSpec((None, sq, hd), lambda b, h: (h, b, 0)),
            pl.BlockSpec((None, past, hd), lambda b, h: (h, b, 0)),
            pl.BlockSpec((None, past, vh), lambda b, h: (h, b, 0)),
            pl.BlockSpec((None, sq, hd), lambda b, h: (h, b, 0)),
            pl.BlockSpec((None, sq, vh), lambda b, h: (h, b, 0)),
        ],
        out_specs=pl.BlockSpec((sq, vh), lambda b, h: (b, h)),
        out_shape=jax.ShapeDtypeStruct((m, heads * vh), BF16),
        compiler_params=_cparams("parallel", "parallel"),
        name="attn_sample",
    )(q, kc, vc, kn, vn)


def _proj_res_ln_kernel(a_ref, w_ref, x_ref, g_ref, b_ref, o_ref, *, alpha):
    y = jnp.dot(a_ref[...], w_ref[...], preferred_element_type=F32)
    o_ref[...] = _layer_norm(alpha * x_ref[...] + y, g_ref[...], b_ref[...])


def _proj_res_ln(a, w, x, g, b, *, alpha):
    m, k = a.shape
    d = w.shape[1]
    bm = _tile(m, 512)
    row = lambda i: (i, 0)
    full = lambda i: (0, 0)
    return pl.pallas_call(
        functools.partial(_proj_res_ln_kernel, alpha=alpha),
        grid=(m // bm,),
        in_specs=[
            pl.BlockSpec((bm, k), row),
            pl.BlockSpec((k, d), full),
            pl.BlockSpec((bm, d), row),
            pl.BlockSpec((1, d), full),
            pl.BlockSpec((1, d), full),
        ],
        out_specs=pl.BlockSpec((bm, d), row),
        out_shape=jax.ShapeDtypeStruct((m, d), F32),
        compiler_params=_cparams("parallel"),
        name="proj_res_ln",
    )(a, w, x, g, b)


def _pool_kernel(x_ref, prev_ref, hist_ref, w_ref, sc_ref, g_ref, b_ref, o_ref, buf, y_scr,
                 *, bm, gw, pos0, alpha):
    i = pl.program_id(1)
    x = x_ref[...]
    halo = jnp.where(i == 0, hist_ref[...], prev_ref[...])
    buf[0:POOL_HALO, :] = halo
    buf[POOL_HALO:POOL_HALO + bm, :] = x
    pos = pos0 + i * bm + lax.broadcasted_iota(jnp.int32, (bm, 1), 0)
    for gi, w in enumerate(POOL_WINDOWS):
        lo, hi = gi * gw, (gi + 1) * gw
        tot = x[:, lo:hi]
        for j in range(1, w):
            tot = tot + buf[POOL_HALO - j:POOL_HALO - j + bm, lo:hi]
        cnt = jnp.minimum(pos + 1, w).astype(F32)
        dlt = (tot / cnt - x[:, lo:hi]).astype(BF16)
        y_scr[:, lo:hi] = jnp.dot(dlt, w_ref[gi], preferred_element_type=F32)
    y = y_scr[...] * sc_ref[...]
    o_ref[...] = _layer_norm(alpha * x + y, g_ref[...], b_ref[...])


def _pool_layer(x, hist, w, sc, g, b, *, pos0, alpha):
    bsz, s, d = x.shape
    gw = d // len(POOL_WINDOWS)
    bm = _tile(s, 512)
    assert bm % POOL_HALO == 0
    hb = bm // POOL_HALO
    kern = functools.partial(_pool_kernel, bm=bm, gw=gw, pos0=pos0, alpha=alpha)
    vec = lambda bb, i: (0, 0)
    return pl.pallas_call(
        kern,
        grid=(bsz, s // bm),
        in_specs=[
            pl.BlockSpec((None, bm, d), lambda bb, i: (bb, i, 0)),
            pl.BlockSpec((None, POOL_HALO, d), lambda bb, i: (bb, jnp.maximum(i * hb - 1, 0), 0)),
            pl.BlockSpec((None, POOL_HALO, d), lambda bb, i: (bb, 0, 0)),
            pl.BlockSpec(w.shape, lambda bb, i: (0, 0, 0)),
            pl.BlockSpec((1, d), vec),
            pl.BlockSpec((1, d), vec),
            pl.BlockSpec((1, d), vec),
        ],
        out_specs=pl.BlockSpec((None, bm, d), lambda bb, i: (bb, i, 0)),
        out_shape=jax.ShapeDtypeStruct((bsz, s, d), F32),
        scratch_shapes=[
            pltpu.VMEM((bm + POOL_HALO, d), F32),
            pltpu.VMEM((bm, d), F32),
        ],
        compiler_params=_cparams("parallel", "arbitrary"),
        name="pool_mix",
    )(x, x, hist, w, sc, g, b)


def _swiglu_tile(tv_ref, x_ref, wg_ref, wu_ref, wd_ref, o_ref, xb_scr, wgb, wub, wdb, *, bm, sb):
    i = pl.program_id(0)
    f = pl.program_id(1)
    valid = tv_ref[i]

    @pl.when(f == 0)
    def _():
        o_ref[...] = jnp.zeros(o_ref.shape, F32)
        xb_scr[...] = x_ref[...].astype(BF16)

    def rows(lo, hi, wg, wu, wd):
        xs = xb_scr[lo:hi, :]
        gte = jnp.dot(xs, wg, preferred_element_type=F32)
        up = jnp.dot(xs, wu, preferred_element_type=F32)
        hid = (gte * jax.nn.sigmoid(gte) * up).astype(BF16)
        o_ref[lo:hi, :] += jnp.dot(hid, wd, preferred_element_type=F32)

    full = valid > bm - sb

    @pl.when(full)
    def _():
        rows(0, bm, wg_ref[...].astype(BF16), wu_ref[...].astype(BF16), wd_ref[...].astype(BF16))

    @pl.when((valid > 0) & jnp.logical_not(full))
    def _():
        wgb[...] = wg_ref[...].astype(BF16)
        wub[...] = wu_ref[...].astype(BF16)
        wdb[...] = wd_ref[...].astype(BF16)

    for s in range(bm // sb - 1):
        @pl.when((s * sb < valid) & jnp.logical_not(full))
        def _():
            rows(s * sb, (s + 1) * sb, wgb[...], wub[...], wdb[...])


def _ffn_dense_kernel(te_ref, tv_ref, x_ref, wg_ref, wu_ref, wd_ref, g_ref, b_ref, o_ref,
                      xb_scr, wgb, wub, wdb, *, bm, sb, alpha):
    del te_ref
    _swiglu_tile(tv_ref, x_ref, wg_ref, wu_ref, wd_ref, o_ref, xb_scr, wgb, wub, wdb, bm=bm, sb=sb)

    @pl.when(pl.program_id(1) == pl.num_programs(1) - 1)
    def _():
        o_ref[...] = _layer_norm(alpha * x_ref[...] + o_ref[...], g_ref[...], b_ref[...])


def _ffn_expert_kernel(te_ref, tv_ref, x_ref, wg_ref, wu_ref, wd_ref, o_ref,
                       xb_scr, wgb, wub, wdb, *, bm, sb):
    del te_ref
    _swiglu_tile(tv_ref, x_ref, wg_ref, wu_ref, wd_ref, o_ref, xb_scr, wgb, wub, wdb, bm=bm, sb=sb)


def _ffn_call(x, wg, wu, wd, layer, tile_expert, tile_valid, *, bm, ln=None, alpha=None):
    r, d = x.shape
    nf_total = wg.shape[3]
    bf = _tile(nf_total, 256)
    nf = nf_total // bf
    sb = _tile(bm, 256)
    nt = r // bm

    def chunk(i, f, te, tv):
        return jnp.where(tv[i] > 0, f, nf - 1)

    in_specs = [
        pl.BlockSpec((bm, d), lambda i, f, te, tv: (i, 0), pipeline_mode=pl.Buffered(1)),
        pl.BlockSpec((None, None, d, bf), lambda i, f, te, tv: (layer, te[i], 0, chunk(i, f, te, tv))),
        pl.BlockSpec((None, None, d, bf), lambda i, f, te, tv: (layer, te[i], 0, chunk(i, f, te, tv))),
        pl.BlockSpec((None, None, bf, d), lambda i, f, te, tv: (layer, te[i], chunk(i, f, te, tv), 0)),
    ]
    args = [x, wg, wu, wd]
    if ln is not None:
        in_specs += [pl.BlockSpec((1, d), lambda i, f, te, tv: (0, 0))] * 2
        args += list(ln)
        kern = functools.partial(_ffn_dense_kernel, bm=bm, sb=sb, alpha=alpha)
    else:
        kern = functools.partial(_ffn_expert_kernel, bm=bm, sb=sb)
    return pl.pallas_call(
        kern,
        grid_spec=pltpu.PrefetchScalarGridSpec(
            num_scalar_prefetch=2,
            grid=(nt, nf),
            in_specs=in_specs,
            out_specs=pl.BlockSpec((bm, d), lambda i, f, te, tv: (i, 0)),
            scratch_shapes=[pltpu.VMEM((bm, d), BF16), pltpu.VMEM((d, bf), BF16),
                            pltpu.VMEM((d, bf), BF16), pltpu.VMEM((bf, d), BF16)],
        ),
        out_shape=jax.ShapeDtypeStruct((r, d), F32),
        compiler_params=_cparams("parallel", "arbitrary"),
        name="swiglu_dense" if ln is not None else "swiglu_experts",
    )(tile_expert, tile_valid, *args)


def _router_kernel(x_ref, wh_ref, wl_ref, o_ref, *, ne):
    x = x_ref[...]
    xh = x.astype(BF16)
    xl = (x - xh.astype(F32)).astype(BF16)
    wh = wh_ref[...]
    lg = (jnp.dot(xh, wh, preferred_element_type=F32)
          + (jnp.dot(xl, wh, preferred_element_type=F32)
             + jnp.dot(xh, wl_ref[...], preferred_element_type=F32)))
    lane = lax.broadcasted_iota(jnp.int32, lg.shape, 1)
    lg = jnp.where(lane < ne, lg, -jnp.inf)
    m1 = jnp.max(lg, axis=-1, keepdims=True)
    i1 = jnp.min(jnp.where(lg == m1, lane, V7X_LANES), axis=-1, keepdims=True)
    lg2 = jnp.where(lane == i1, -jnp.inf, lg)
    m2 = jnp.max(lg2, axis=-1, keepdims=True)
    i2 = jnp.min(jnp.where(lg2 == m2, lane, V7X_LANES), axis=-1, keepdims=True)
    e = jnp.exp(m2 - m1)
    den = 1.0 + e
    g1 = 1.0 / den
    g2 = e / den
    out = jnp.where(lane == 0, i1.astype(F32),
                    jnp.where(lane == 1, i2.astype(F32),
                              jnp.where(lane == 2, g1, jnp.where(lane == 3, g2, 0.0))))
    o_ref[...] = out


def _router(x, wh, wl, *, ne):
    m, d = x.shape
    bm = _tile(m, 512)
    return pl.pallas_call(
        functools.partial(_router_kernel, ne=ne),
        grid=(m // bm,),
        in_specs=[
            pl.BlockSpec((bm, d), lambda i: (i, 0)),
            pl.BlockSpec(wh.shape, lambda i: (0, 0)),
            pl.BlockSpec(wl.shape, lambda i: (0, 0)),
        ],
        out_specs=pl.BlockSpec((bm, V7X_LANES), lambda i: (i, 0)),
        out_shape=jax.ShapeDtypeStruct((m, V7X_LANES), F32),
        compiler_params=_cparams("parallel"),
        name="router",
    )(x, wh, wl)


DMA_ISSUE_UNROLL = 8


def _row_copy(src, src_row, dst, dst_row, sem):
    return pltpu.make_async_copy(src.at[pl.ds(src_row, 1)], dst.at[pl.ds(dst_row, 1)], sem)


def _dispatch_kernel(pos_ref, x_ref, init_hbm, xs_hbm, sem, *, bt):
    del init_hbm
    base = pl.program_id(0) * bt

    def issue(r, carry):
        for k in range(TOP_K):
            _row_copy(x_ref, r, xs_hbm, pos_ref[(base + r) * TOP_K + k], sem).start()
        return carry

    def drain(r, carry):
        for k in range(TOP_K):
            _row_copy(x_ref, 0, xs_hbm, 0, sem).wait()
        return carry

    lax.fori_loop(0, bt, issue, 0, unroll=DMA_ISSUE_UNROLL)
    lax.fori_loop(0, bt, drain, 0, unroll=DMA_ISSUE_UNROLL)


def _dispatch(x, pos, init):
    m, d = x.shape
    rows = init.shape[0]
    bt = _tile(m, 256)
    return pl.pallas_call(
        functools.partial(_dispatch_kernel, bt=bt),
        grid_spec=pltpu.PrefetchScalarGridSpec(
            num_scalar_prefetch=1,
            grid=(m // bt,),
            in_specs=[pl.BlockSpec((bt, d), lambda i, p: (i, 0)), pl.BlockSpec(memory_space=pl.ANY)],
            out_specs=pl.BlockSpec(memory_space=pl.ANY),
            scratch_shapes=[pltpu.SemaphoreType.DMA(())],
        ),
        out_shape=jax.ShapeDtypeStruct((rows, d), x.dtype),
        input_output_aliases={2: 0},
        compiler_params=_cparams("arbitrary"),
        name="moe_dispatch",
    )(pos, x, init)


def _combine_kernel(pos_ref, ys_hbm, x_ref, rt_ref, g_ref, b_ref, o_ref, buf, sem, *, bt, alpha):
    base = pl.program_id(0) * bt

    def issue(r, carry):
        for k in range(TOP_K):
            _row_copy(ys_hbm, pos_ref[(base + r) * TOP_K + k], buf.at[k], r, sem).start()
        return carry

    def drain(r, carry):
        for k in range(TOP_K):
            _row_copy(ys_hbm, 0, buf.at[k], 0, sem).wait()
        return carry

    lax.fori_loop(0, bt, issue, 0, unroll=DMA_ISSUE_UNROLL)
    lax.fori_loop(0, bt, drain, 0, unroll=DMA_ISSUE_UNROLL)
    rt = rt_ref[...]
    y = buf[0] * rt[:, TOP_K:TOP_K + 1]
    for k in range(1, TOP_K):
        y = y + buf[k] * rt[:, TOP_K + k:TOP_K + k + 1]
    o_ref[...] = _layer_norm(alpha * x_ref[...] + y, g_ref[...], b_ref[...])


def _combine(ys, pos, x, route, g, b, *, alpha):
    m, d = x.shape
    bt = _tile(m, 256)
    row = lambda i, p: (i, 0)
    full = lambda i, p: (0, 0)
    return pl.pallas_call(
        functools.partial(_combine_kernel, bt=bt, alpha=alpha),
        grid_spec=pltpu.PrefetchScalarGridSpec(
            num_scalar_prefetch=1,
            grid=(m // bt,),
            in_specs=[
                pl.BlockSpec(memory_space=pl.ANY),
                pl.BlockSpec((bt, d), row),
                pl.BlockSpec((bt, V7X_LANES), row),
                pl.BlockSpec((1, d), full),
                pl.BlockSpec((1, d), full),
            ],
            out_specs=pl.BlockSpec((bt, d), row),
            scratch_shapes=[pltpu.VMEM((TOP_K, bt, d), F32), pltpu.SemaphoreType.DMA(())],
        ),
        out_shape=jax.ShapeDtypeStruct((m, d), F32),
        compiler_params=_cparams("arbitrary"),
        name="moe_combine",
    )(pos, ys, x, route, g, b)


def _moe_plan(idx, ne, bm, nt):
    e_flat = idx.reshape(-1)
    onehot = (e_flat[None, :] == jnp.arange(ne, dtype=jnp.int32)[:, None]).astype(jnp.int32)
    csum = jnp.cumsum(onehot, axis=1)
    rank = jnp.sum(onehot * csum, axis=0) - 1
    counts = csum[:, -1]
    tiles_e = (counts + bm - 1) // bm
    tile_end = jnp.cumsum(tiles_e)
    tile_start = tile_end - tiles_e
    pos = (tile_start[e_flat] * bm + rank).astype(jnp.int32)
    t = jnp.arange(nt, dtype=jnp.int32)
    te = jnp.minimum(jnp.sum((t[:, None] >= tile_end[None, :]).astype(jnp.int32), axis=1), ne - 1)
    tv = jnp.clip(counts[te] - (t - tile_start[te]) * bm, 0, bm).astype(jnp.int32)
    return pos, te, tv


FFN_ROW_TILE = 1024


def _moe_layer(xs, wr_hi, wr_lo, wg, wu, wd, layer, g, b, *, ne, alpha):
    d = xs[0].shape[1]
    routes = [_router(x, wr_hi, wr_lo, ne=ne) for x in xs]
    idx = jnp.concatenate([r[:, :TOP_K] for r in routes], axis=0).astype(jnp.int32)
    bm = FFN_ROW_TILE
    nt = (idx.shape[0] * TOP_K) // bm + ne
    pos, te, tv = _moe_plan(idx, ne, bm, nt)
    bounds = [0]
    for x in xs:
        bounds.append(bounds[-1] + x.shape[0] * TOP_K)
    slots = [pos[lo:hi] for lo, hi in zip(bounds[:-1], bounds[1:])]
    sorted_rows = jnp.zeros((nt * bm, d), F32)
    for x, p in zip(xs, slots):
        sorted_rows = _dispatch(x, p, sorted_rows)
    ys = _ffn_call(sorted_rows, wg, wu, wd, layer, te, tv, bm=bm)
    return [_combine(ys, p, x, r, g, b, alpha=alpha) for x, p, r in zip(xs, slots, routes)]


def _dense_ffn_layer(x, wg, wu, wd, layer, g, b, *, alpha):
    m = x.shape[0]
    bm = _tile(m, FFN_ROW_TILE)
    nt = m // bm
    te = jnp.zeros((nt,), jnp.int32)
    tv = jnp.full((nt,), bm, jnp.int32)
    return _ffn_call(x, wg[:, None], wu[:, None], wd[:, None], layer, te, tv,
                     bm=bm, ln=(g, b), alpha=alpha)


def _rope_tables(pos, rope, scale):
    inv = ROPE_BASE ** (-jnp.arange(0, rope, 2, dtype=F32) / rope)
    ang = pos.astype(F32)[:, None] * inv[None, :]
    cos, sin = jnp.cos(ang), jnp.sin(ang)
    t1 = jnp.concatenate([cos, cos, sin, sin], axis=1)
    ones = jnp.ones((pos.shape[0], V7X_LANES), F32)
    t2 = jnp.concatenate([ones, t1], axis=1) * scale
    return t1, t2


def _swap_halves(w):
    half = w.shape[-1] // 2
    return jnp.concatenate([-w[..., half:], w[..., :half]], axis=-1)


def _prep_mla(w_dq, w_uq, w_dkv, w_uk, w_uv, w_o, *, kl, heads, nope, rope):
    ql = w_dq.shape[1]
    kr = w_dkv[:, kl:]
    w1 = jnp.concatenate([w_dq, w_dkv[:, :kl], kr, _swap_halves(kr)], axis=1).astype(BF16)
    uq = w_uq.reshape(ql, heads, nope + rope)
    qr = uq[..., nope:]
    w2 = jnp.concatenate([uq[..., :nope], qr, _swap_halves(qr)], axis=-1)
    w2 = w2.reshape(ql, heads * (nope + 2 * rope)).astype(BF16)
    wk = w_uk.reshape(kl, -1).astype(BF16)
    wv = w_uv.reshape(kl, -1).astype(BF16)
    return w1, w2, wk, wv, w_o.astype(BF16)


def kernel(x_prompt, x_sample, cache_ckv, cache_krope, state_pool, mla_w_dq, mla_q_norm, mla_w_uq, mla_w_dkv, mla_kv_norm, mla_w_uk, mla_w_uv, mla_w_o, pool_w, pool_scale, ffn_w_gate, ffn_w_up, ffn_w_down, moe_w_router, moe_w_gate, moe_w_up, moe_w_down, ln_mix_g, ln_mix_b, ln_ffn_g, ln_ffn_b):
    bp, sp, d = x_prompt.shape
    bs, ss, _ = x_sample.shape
    past = cache_ckv.shape[2]
    depth = ln_mix_g.shape[0]
    kl, heads, nope = mla_w_uk.shape[1:]
    vh = mla_w_uv.shape[3]
    ql = mla_w_dq.shape[2]
    rope = mla_w_dkv.shape[2] - kl
    hd = nope + 2 * rope
    ne = moe_w_router.shape[2]
    alpha = (2.0 * depth) ** 0.25
    scale = math.log2(math.e) / math.sqrt(nope + rope)
    assert bp == 1 and heads % 2 == 0 and 2 * rope == V7X_LANES and nope == V7X_LANES
    assert past % CHUNK == 0 and state_pool.shape[2] == POOL_HALO - 1

    t1_p, t2_p = _rope_tables(jnp.arange(sp, dtype=jnp.int32), rope, scale)
    t1_s, t2_s = _rope_tables(jnp.tile(past + jnp.arange(ss, dtype=jnp.int32), bs), rope, scale)

    xp = x_prompt.reshape(sp, d)
    xs = x_sample.reshape(bs * ss, d)
    row = lambda v: v.reshape(1, -1)
    ckv_p, kr_p, pool_p, ckv_s, kr_s, pool_s = [], [], [], [], [], []
    blk = _tile(sp, 512)

    for i in range(depth):
        j = i // 2
        if i % 2 == 0:
            w1, w2, wk, wv, wo = _prep_mla(mla_w_dq[j], mla_w_uq[j], mla_w_dkv[j], mla_w_uk[j],
                                           mla_w_uv[j], mla_w_o[j], kl=kl, heads=heads, nope=nope, rope=rope)
            dims = dict(ql=ql, kl=kl, rope=rope, heads=heads, hd=hd)
            up = dict(kl=kl, heads=heads, nope=nope, vh=vh, hd=hd)
            qn, kvn = row(mla_q_norm[j]), row(mla_kv_norm[j])
            lg, lb = row(ln_mix_g[i]), row(ln_mix_b[i])
            q, ckv, kr, ckr = _mla_proj(xp, w1, w2, qn, kvn, t1_p, t2_p, **dims)
            k, v = _kv_up(ckr, wk, wv, **up)
            o = _attn_prompt(q, k, v, blk=blk)
            xp = _proj_res_ln(o, wo, xp, lg, lb, alpha=alpha)
            ckv_p.append(ckv.reshape(bp, sp, kl))
            kr_p.append(kr.reshape(bp, sp, rope))
            q, ckv, kr, ckr = _mla_proj(xs, w1, w2, qn, kvn, t1_s, t2_s, **dims)
            kn, vn = _kv_up(ckr, wk, wv, **up)
            ckr_c = jnp.concatenate([cache_ckv[j], cache_krope[j], cache_krope[j]], axis=-1)
            kc, vc = _kv_up(ckr_c.reshape(bs * past, kl + 2 * rope).astype(BF16), wk, wv, **up)
            o = _attn_sample(q, kc, vc, kn, vn, batch=bs, past=past)
            xs = _proj_res_ln(o, wo, xs, lg, lb, alpha=alpha)
            ckv_s.append(ckv.reshape(bs, ss, kl))
            kr_s.append(kr.reshape(bs, ss, rope))
        else:
            pw = pool_w[j].astype(BF16)
            sc = row(pool_scale[j])
            lg, lb = row(ln_mix_g[i]), row(ln_mix_b[i])
            xp3 = xp.reshape(bp, sp, d)
            xs3 = xs.reshape(bs, ss, d)
            pool_p.append(xp3[:, sp - (POOL_HALO - 1):, :])
            pool_s.append(jnp.concatenate([state_pool[j], xs3], axis=1)[:, ss:, :])
            hist_p = jnp.zeros((bp, POOL_HALO, d), F32)
            hist_s = jnp.concatenate([jnp.zeros((bs, 1, d), F32), state_pool[j]], axis=1)
            xp = _pool_layer(xp3, hist_p, pw, sc, lg, lb, pos0=0, alpha=alpha).reshape(sp, d)
            xs = _pool_layer(xs3, hist_s, pw, sc, lg, lb, pos0=past, alpha=alpha).reshape(bs * ss, d)

        fg, fb = row(ln_ffn_g[i]), row(ln_ffn_b[i])
        if i % 2 == 0:
            xp = _dense_ffn_layer(xp, ffn_w_gate, ffn_w_up, ffn_w_down, j, fg, fb, alpha=alpha)
            xs = _dense_ffn_layer(xs, ffn_w_gate, ffn_w_up, ffn_w_down, j, fg, fb, alpha=alpha)
        else:
            wr = jnp.pad(moe_w_router[j], ((0, 0), (0, V7X_LANES - ne)))
            wr_hi = wr.astype(BF16)
            wr_lo = (wr - wr_hi.astype(F32)).astype(BF16)
            xp, xs = _moe_layer([xp, xs], wr_hi, wr_lo, moe_w_gate, moe_w_up, moe_w_down, j, fg, fb,
                                ne=ne, alpha=alpha)

    return (xp.reshape(bp, sp, d), xs.reshape(bs, ss, d),
            jnp.stack(ckv_p), jnp.stack(kr_p), jnp.stack(pool_p),
            jnp.stack(ckv_s), jnp.stack(kr_s), jnp.stack(pool_s))
```

```python
import functools
import math

import jax
import jax.numpy as jnp
from jax import lax
from jax.experimental import pallas as pl
from jax.experimental.pallas import tpu as pltpu

BF16 = jnp.bfloat16
F32 = jnp.float32

CHUNK = 64
POOL_WINDOWS = (2, 4, 8, 16)
POOL_HALO = 16
ROPE_BASE = 10000.0
LN_EPS = 1e-5
RMS_EPS = 1e-6
TOP_K = 2

V7X_LANES = 128
V7X_VMEM_BUDGET = 56 * 1024 * 1024


def _cparams(*sem):
    return pltpu.CompilerParams(dimension_semantics=sem, vmem_limit_bytes=V7X_VMEM_BUDGET)


def _tile(n, pref):
    if n <= pref:
        return n
    t = pref
    while n % t:
        t //= 2
    assert t >= 8, (n, pref)
    return t


def _layer_norm(z, g, b):
    mu = jnp.mean(z, axis=-1, keepdims=True)
    zc = z - mu
    var = jnp.mean(zc * zc, axis=-1, keepdims=True)
    return zc * lax.rsqrt(var + LN_EPS) * g + b


def _rms(c, g):
    return c * lax.rsqrt(jnp.mean(c * c, axis=-1, keepdims=True) + RMS_EPS) * g


def _mla_proj_kernel(x_ref, w1_ref, w2_ref, qn_ref, kvn_ref, t1_ref, t2_ref,
                     q_ref, ckv_ref, kr_ref, ckr_ref, *, ql, kl, rope, heads, hd):
    xb = x_ref[...].astype(BF16)
    r = jnp.dot(xb, w1_ref[...], preferred_element_type=F32)
    cq = _rms(r[:, :ql], qn_ref[...])
    ckv = _rms(r[:, ql:ql + kl], kvn_ref[...])
    t = r[:, ql + kl:] * t1_ref[...]
    kk = t + pltpu.roll(t, rope, axis=1)
    ckv_ref[...] = ckv
    kr_ref[...] = kk[:, :rope]
    ckr_ref[:, :kl] = ckv.astype(BF16)
    ckr_ref[:, kl:] = kk.astype(BF16)
    cqb = cq.astype(BF16)
    t2 = t2_ref[...]
    for h in range(heads):
        qh = jnp.dot(cqb, w2_ref[:, h * hd:(h + 1) * hd], preferred_element_type=F32)
        q_ref[h] = (qh * t2).astype(BF16)


def _mla_proj(x, w1, w2, qn, kvn, t1, t2, *, ql, kl, rope, heads, hd):
    m, d = x.shape
    bm = _tile(m, 256)
    kern = functools.partial(_mla_proj_kernel, ql=ql, kl=kl, rope=rope, heads=heads, hd=hd)
    row = lambda i: (i, 0)
    full = lambda i: (0, 0)
    return pl.pallas_call(
        kern,
        grid=(m // bm,),
        in_specs=[
            pl.BlockSpec((bm, d), row),
            pl.BlockSpec(w1.shape, full),
            pl.BlockSpec(w2.shape, full),
            pl.BlockSpec((1, ql), full),
            pl.BlockSpec((1, kl), full),
            pl.BlockSpec((bm, 2 * rope), row),
            pl.BlockSpec((bm, hd), row),
        ],
        out_specs=[
            pl.BlockSpec((heads, bm, hd), lambda i: (0, i, 0)),
            pl.BlockSpec((bm, kl), row),
            pl.BlockSpec((bm, rope), row),
            pl.BlockSpec((bm, kl + 2 * rope), row),
        ],
        out_shape=[
            jax.ShapeDtypeStruct((heads, m, hd), BF16),
            jax.ShapeDtypeStruct((m, kl), F32),
            jax.ShapeDtypeStruct((m, rope), F32),
            jax.ShapeDtypeStruct((m, kl + 2 * rope), BF16),
        ],
        compiler_params=_cparams("parallel"),
        name="mla_proj",
    )(x, w1, w2, qn, kvn, t1, t2)


def _kv_up_kernel(ckr_ref, wk_ref, wv_ref, k_ref, v_ref, *, kl, heads, nope, vh):
    c = ckr_ref[:, :kl]
    kk = ckr_ref[:, kl:]
    for p in range(heads // 2):
        kn = jnp.dot(c, wk_ref[:, 2 * p * nope:(2 * p + 2) * nope], preferred_element_type=F32)
        vv = jnp.dot(c, wv_ref[:, 2 * p * vh:(2 * p + 2) * vh], preferred_element_type=F32)
        for s in range(2):
            h = 2 * p + s
            k_ref[h, :, :nope] = kn[:, s * nope:(s + 1) * nope].astype(BF16)
            k_ref[h, :, nope:] = kk
            v_ref[h] = vv[:, s * vh:(s + 1) * vh].astype(BF16)


def _kv_up(ckr, wk, wv, *, kl, heads, nope, vh, hd):
    m = ckr.shape[0]
    bm = _tile(m, 512)
    kern = functools.partial(_kv_up_kernel, kl=kl, heads=heads, nope=nope, vh=vh)
    return pl.pallas_call(
        kern,
        grid=(m // bm,),
        in_specs=[
            pl.BlockSpec((bm, ckr.shape[1]), lambda i: (i, 0)),
            pl.BlockSpec(wk.shape, lambda i: (0, 0)),
            pl.BlockSpec(wv.shape, lambda i: (0, 0)),
        ],
        out_specs=[
            pl.BlockSpec((heads, bm, hd), lambda i: (0, i, 0)),
            pl.BlockSpec((heads, bm, vh), lambda i: (0, i, 0)),
        ],
        out_shape=[
            jax.ShapeDtypeStruct((heads, m, hd), BF16),
            jax.ShapeDtypeStruct((heads, m, vh), BF16),
        ],
        compiler_params=_cparams("parallel"),
        name="kv_up",
    )(ckr, wk, wv)


def _attn_prompt_kernel(q_ref, k_ref, v_ref, o_ref, m_scr, l_scr, acc_scr, *, blk, hp, qs, vh):
    i = pl.program_id(1)
    m_scr[...] = jnp.full(m_scr.shape, -jnp.inf, F32)
    l_scr[...] = jnp.zeros(l_scr.shape, F32)
    acc_scr[...] = jnp.zeros(acc_scr.shape, F32)

    def update(a, u, k, v, diagonal):
        rows = slice(u * blk, (u + 1) * blk)
        s = lax.dot_general(q_ref[a, rows, :], k, (((1,), (1,)), ((), ())), preferred_element_type=F32)
        if diagonal:
            qc = lax.broadcasted_iota(jnp.int32, s.shape, 0) // CHUNK
            kc = lax.broadcasted_iota(jnp.int32, s.shape, 1) // CHUNK
            s = jnp.where(kc <= qc, s, -jnp.inf)
        cols = [s[:, c * V7X_LANES:(c + 1) * V7X_LANES] for c in range(s.shape[1] // V7X_LANES)]
        mx = cols[0]
        for c in cols[1:]:
            mx = jnp.maximum(mx, c)
        m_prev = m_scr[a, rows, :]
        m_new = jnp.maximum(m_prev, jnp.max(mx, axis=-1, keepdims=True))
        alpha = jnp.exp2(m_prev - m_new)
        ps = [jnp.exp2(c - m_new) for c in cols]
        lsum = ps[0]
        for pc in ps[1:]:
            lsum = lsum + pc
        l_scr[a, rows, :] = alpha * l_scr[a, rows, :] + lsum
        p = jnp.concatenate([pc.astype(BF16) for pc in ps], axis=1)
        acc_scr[a, rows, :] = alpha * acc_scr[a, rows, :] + jnp.dot(p, v, preferred_element_type=F32)
        m_scr[a, rows, :] = m_new

    def key_block(start, width, subs, diagonal):
        for a in range(hp):
            k = k_ref[a, pl.ds(start, width), :]
            v = v_ref[a, pl.ds(start, width), :]
            for u in subs:
                update(a, u, k, v, diagonal)

    def fully_visible(jj, carry):
        key_block(pl.multiple_of(jj * (blk * qs), blk * qs), blk * qs, range(qs), False)
        return carry

    lax.fori_loop(0, i, fully_visible, 0)
    own = pl.multiple_of(i * (blk * qs), blk * qs)
    for u in range(qs):
        if u:
            key_block(own, u * blk, [u], False)
        key_block(pl.multiple_of(own + u * blk, blk), blk, [u], True)
    for a in range(hp):
        l = jnp.sum(l_scr[a], axis=-1, keepdims=True)
        o_ref[:, a * vh:(a + 1) * vh] = (acc_scr[a] / l).astype(o_ref.dtype)


def _attn_prompt(q, k, v, *, blk, hp=2, qs=4):
    heads, s, hd = q.shape
    vh = v.shape[2]
    bq = blk * qs
    assert heads % hp == 0 and vh == V7X_LANES and s % bq == 0
    kern = functools.partial(_attn_prompt_kernel, blk=blk, hp=hp, qs=qs, vh=vh)
    once = pl.Buffered(1)
    return pl.pallas_call(
        kern,
        grid=(heads // hp, s // bq),
        in_specs=[
            pl.BlockSpec((hp, bq, hd), lambda h, i: (h, i, 0)),
            pl.BlockSpec((hp, s, hd), lambda h, i: (h, 0, 0), pipeline_mode=once),
            pl.BlockSpec((hp, s, vh), lambda h, i: (h, 0, 0), pipeline_mode=once),
        ],
        out_specs=pl.BlockSpec((bq, hp * vh), lambda h, i: (i, h)),
        out_shape=jax.ShapeDtypeStruct((s, heads * vh), BF16),
        scratch_shapes=[
            pltpu.VMEM((hp, bq, V7X_LANES), F32),
            pltpu.VMEM((hp, bq, V7X_LANES), F32),
            pltpu.VMEM((hp, bq, vh), F32),
        ],
        compiler_params=_cparams("parallel", "arbitrary"),
        name="attn_prompt",
    )(q, k, v)


def _attn_sample_kernel(q_ref, kc_ref, vc_ref, kn_ref, vn_ref, o_ref, *, past, hp, vh):
    dn = (((1,), (1,)), ((), ()))
    for a in range(hp):
        q = q_ref[a]
        s1 = lax.dot_general(q, kc_ref[a], dn, preferred_element_type=F32)
        s2 = lax.dot_general(q, kn_ref[a], dn, preferred_element_type=F32)
        qc1 = (past + lax.broadcasted_iota(jnp.int32, s1.shape, 0)) // CHUNK
        kc1 = lax.broadcasted_iota(jnp.int32, s1.shape, 1) // CHUNK
        s1 = jnp.where(kc1 <= qc1, s1, -jnp.inf)
        qc2 = (past + lax.broadcasted_iota(jnp.int32, s2.shape, 0)) // CHUNK
        kc2 = (past + lax.broadcasted_iota(jnp.int32, s2.shape, 1)) // CHUNK
        s2 = jnp.where(kc2 <= qc2, s2, -jnp.inf)
        m = jnp.maximum(jnp.max(s1, axis=-1, keepdims=True), jnp.max(s2, axis=-1, keepdims=True))
        p1 = jnp.exp2(s1 - m)
        p2 = jnp.exp2(s2 - m)
        l = jnp.sum(p1, axis=-1, keepdims=True) + jnp.sum(p2, axis=-1, keepdims=True)
        o = (jnp.dot(p1.astype(BF16), vc_ref[a], preferred_element_type=F32)
             + jnp.dot(p2.astype(BF16), vn_ref[a], preferred_element_type=F32))
        o_ref[:, a * vh:(a + 1) * vh] = (o / l).astype(o_ref.dtype)


def _attn_sample(q, kc, vc, kn, vn, *, batch, past):
    heads, m, hd = q.shape
    sq = m // batch
    vh = vc.shape[2]
    hp = math.gcd(heads, 4)
    kern = functools.partial(_attn_sample_kernel, past=past, hp=hp, vh=vh)
    return pl.pallas_call(
        kern,
        grid=(batch, heads // hp),
        in_specs=[
            pl.BlockSpec((hp, sq, hd), lambda b, h: (h, b, 0)),
            pl.BlockSpec((hp, past, hd), lambda b, h: (h, b, 0)),
            pl.BlockSpec((hp, past, vh), lambda b, h: (h, b, 0)),
            pl.BlockSpec((hp, sq, hd), lambda b, h: (h, b, 0)),
            pl.BlockSpec((hp, sq, vh), lambda b, h: (h, b, 0)),
        ],
        out_specs=pl.BlockSpec((sq, hp * vh), lambda b, h: (b, h)),
        out_shape=jax.ShapeDtypeStruct((m, heads * vh), BF16),
        compiler_params=_cparams("parallel", "parallel"),
        name="attn_sample",
    )(q, kc, vc, kn, vn)


def _proj_res_ln_kernel(a_ref, w_ref, x_ref, g_ref, b_ref, o_ref, *, alpha):
    y = jnp.dot(a_ref[...], w_ref[...], preferred_element_type=F32)
    o_ref[...] = _layer_norm(alpha * x_ref[...] + y, g_ref[...], b_ref[...])


def _proj_res_ln(a, w, x, g, b, *, alpha):
    m, k = a.shape
    d = w.shape[1]
    bm = _tile(m, 512)
    row = lambda i: (i, 0)
    full = lambda i: (0, 0)
    return pl.pallas_call(
        functools.partial(_proj_res_ln_kernel, alpha=alpha),
        grid=(m // bm,),
        in_specs=[
            pl.BlockSpec((bm, k), row),
            pl.BlockSpec((k, d), full),
            pl.BlockSpec((bm, d), row),
            pl.BlockSpec((1, d), full),
            pl.BlockSpec((1, d), full),
        ],
        out_specs=pl.BlockSpec((bm, d), row),
        out_shape=jax.ShapeDtypeStruct((m, d), F32),
        compiler_params=_cparams("parallel"),
        name="proj_res_ln",
    )(a, w, x, g, b)


def _pool_kernel(x_ref, prev_ref, hist_ref, w_ref, sc_ref, g_ref, b_ref, o_ref, buf, y_scr,
                 *, bm, gw, pos0, alpha):
    i = pl.program_id(1)
    x = x_ref[...]
    halo = jnp.where(i == 0, hist_ref[...], prev_ref[...])
    buf[0:POOL_HALO, :] = halo
    buf[POOL_HALO:POOL_HALO + bm, :] = x
    pos = pos0 + i * bm + lax.broadcasted_iota(jnp.int32, (bm, 1), 0)
    for gi, w in enumerate(POOL_WINDOWS):
        lo, hi = gi * gw, (gi + 1) * gw
        tot = x[:, lo:hi]
        for j in range(1, w):
            tot = tot + buf[POOL_HALO - j:POOL_HALO - j + bm, lo:hi]
        cnt = jnp.minimum(pos + 1, w).astype(F32)
        dlt = (tot / cnt - x[:, lo:hi]).astype(BF16)
        y_scr[:, lo:hi] = jnp.dot(dlt, w_ref[gi], preferred_element_type=F32)
    y = y_scr[...] * sc_ref[...]
    o_ref[...] = _layer_norm(alpha * x + y, g_ref[...], b_ref[...])


def _pool_layer(x, hist, w, sc, g, b, *, pos0, alpha):
    bsz, s, d = x.shape
    gw = d // len(POOL_WINDOWS)
    bm = _tile(s, 512)
    assert bm % POOL_HALO == 0
    hb = bm // POOL_HALO
    kern = functools.partial(_pool_kernel, bm=bm, gw=gw, pos0=pos0, alpha=alpha)
    vec = lambda bb, i: (0, 0)
    return pl.pallas_call(
        kern,
        grid=(bsz, s // bm),
        in_specs=[
            pl.BlockSpec((None, bm, d), lambda bb, i: (bb, i, 0)),
            pl.BlockSpec((None, POOL_HALO, d), lambda bb, i: (bb, jnp.maximum(i * hb - 1, 0), 0)),
            pl.BlockSpec((None, POOL_HALO, d), lambda bb, i: (bb, 0, 0)),
            pl.BlockSpec(w.shape, lambda bb, i: (0, 0, 0)),
            pl.BlockSpec((1, d), vec),
            pl.BlockSpec((1, d), vec),
            pl.BlockSpec((1, d), vec),
        ],
        out_specs=pl.BlockSpec((None, bm, d), lambda bb, i: (bb, i, 0)),
        out_shape=jax.ShapeDtypeStruct((bsz, s, d), F32),
        scratch_shapes=[
            pltpu.VMEM((bm + POOL_HALO, d), F32),
            pltpu.VMEM((bm, d), F32),
        ],
        compiler_params=_cparams("parallel", "arbitrary"),
        name="pool_mix",
    )(x, x, hist, w, sc, g, b)


def _swiglu_tile(tv_ref, x_ref, wg_ref, wu_ref, wd_ref, o_ref, xb_scr, wgb, wub, wdb, *, bm, sb):
    i = pl.program_id(0)
    f = pl.program_id(1)
    valid = tv_ref[i]

    @pl.when(f == 0)
    def _():
        o_ref[...] = jnp.zeros(o_ref.shape, F32)
        xb_scr[...] = x_ref[...].astype(BF16)

    def rows(lo, hi, wg, wu, wd):
        xs = xb_scr[lo:hi, :]
        gte = jnp.dot(xs, wg, preferred_element_type=F32)
        up = jnp.dot(xs, wu, preferred_element_type=F32)
        hid = (gte * jax.nn.sigmoid(gte) * up).astype(BF16)
        o_ref[lo:hi, :] += jnp.dot(hid, wd, preferred_element_type=F32)

    full = valid > bm - sb

    @pl.when(full)
    def _():
        rows(0, bm, wg_ref[...].astype(BF16), wu_ref[...].astype(BF16), wd_ref[...].astype(BF16))

    @pl.when((valid > 0) & jnp.logical_not(full))
    def _():
        wgb[...] = wg_ref[...].astype(BF16)
        wub[...] = wu_ref[...].astype(BF16)
        wdb[...] = wd_ref[...].astype(BF16)

    for s in range(bm // sb - 1):
        @pl.when((s * sb < valid) & jnp.logical_not(full))
        def _():
            rows(s * sb, (s + 1) * sb, wgb[...], wub[...], wdb[...])


def _ffn_dense_kernel(te_ref, tv_ref, x_ref, wg_ref, wu_ref, wd_ref, g_ref, b_ref, o_ref,
                      xb_scr, wgb, wub, wdb, *, bm, sb, alpha):
    del te_ref
    _swiglu_tile(tv_ref, x_ref, wg_ref, wu_ref, wd_ref, o_ref, xb_scr, wgb, wub, wdb, bm=bm, sb=sb)

    @pl.when(pl.program_id(1) == pl.num_programs(1) - 1)
    def _():
        rc = _tile(bm, LN_ROW_CHUNK)

        def norm_rows(c, carry):
            r = pl.ds(pl.multiple_of(c * rc, rc), rc)
            o_ref[r, :] = _layer_norm(alpha * x_ref[r, :] + o_ref[r, :], g_ref[...], b_ref[...])
            return carry

        lax.fori_loop(0, bm // rc, norm_rows, 0)


def _ffn_expert_kernel(te_ref, tv_ref, x_ref, wg_ref, wu_ref, wd_ref, o_ref,
                       xb_scr, wgb, wub, wdb, *, bm, sb):
    del te_ref
    _swiglu_tile(tv_ref, x_ref, wg_ref, wu_ref, wd_ref, o_ref, xb_scr, wgb, wub, wdb, bm=bm, sb=sb)


def _ffn_call(x, wg, wu, wd, layer, tile_expert, tile_valid, *, bm, ln=None, alpha=None):
    r, d = x.shape
    nf_total = wg.shape[3]
    bf = _tile(nf_total, 256)
    nf = nf_total // bf
    sb = _tile(bm, 256)
    nt = r // bm

    def chunk(i, f, te, tv):
        return jnp.where(tv[i] > 0, f, nf - 1)

    in_specs = [
        pl.BlockSpec((bm, d), lambda i, f, te, tv: (i, 0)),
        pl.BlockSpec((None, None, d, bf), lambda i, f, te, tv: (layer, te[i], 0, chunk(i, f, te, tv))),
        pl.BlockSpec((None, None, d, bf), lambda i, f, te, tv: (layer, te[i], 0, chunk(i, f, te, tv))),
        pl.BlockSpec((None, None, bf, d), lambda i, f, te, tv: (layer, te[i], chunk(i, f, te, tv), 0)),
    ]
    args = [x, wg, wu, wd]
    if ln is not None:
        in_specs += [pl.BlockSpec((1, d), lambda i, f, te, tv: (0, 0))] * 2
        args += list(ln)
        kern = functools.partial(_ffn_dense_kernel, bm=bm, sb=sb, alpha=alpha)
    else:
        kern = functools.partial(_ffn_expert_kernel, bm=bm, sb=sb)
    return pl.pallas_call(
        kern,
        grid_spec=pltpu.PrefetchScalarGridSpec(
            num_scalar_prefetch=2,
            grid=(nt, nf),
            in_specs=in_specs,
            out_specs=pl.BlockSpec((bm, d), lambda i, f, te, tv: (i, 0)),
            scratch_shapes=[pltpu.VMEM((bm, d), BF16), pltpu.VMEM((d, bf), BF16),
                            pltpu.VMEM((d, bf), BF16), pltpu.VMEM((bf, d), BF16)],
        ),
        out_shape=jax.ShapeDtypeStruct((r, d), F32),
        compiler_params=_cparams("parallel", "arbitrary"),
        name="swiglu_dense" if ln is not None else "swiglu_experts",
    )(tile_expert, tile_valid, *args)


def _router_kernel(x_ref, wh_ref, wl_ref, o_ref, *, ne):
    x = x_ref[...]
    xh = x.astype(BF16)
    xl = (x - xh.astype(F32)).astype(BF16)
    wh = wh_ref[...]
    lg = (jnp.dot(xh, wh, preferred_element_type=F32)
          + (jnp.dot(xl, wh, preferred_element_type=F32)
             + jnp.dot(xh, wl_ref[...], preferred_element_type=F32)))
    lane = lax.broadcasted_iota(jnp.int32, lg.shape, 1)
    lg = jnp.where(lane < ne, lg, -jnp.inf)
    m1 = jnp.max(lg, axis=-1, keepdims=True)
    i1 = jnp.min(jnp.where(lg == m1, lane, V7X_LANES), axis=-1, keepdims=True)
    lg2 = jnp.where(lane == i1, -jnp.inf, lg)
    m2 = jnp.max(lg2, axis=-1, keepdims=True)
    i2 = jnp.min(jnp.where(lg2 == m2, lane, V7X_LANES), axis=-1, keepdims=True)
    e = jnp.exp(m2 - m1)
    den = 1.0 + e
    g1 = 1.0 / den
    g2 = e / den
    out = jnp.where(lane == 0, i1.astype(F32),
                    jnp.where(lane == 1, i2.astype(F32),
                              jnp.where(lane == 2, g1, jnp.where(lane == 3, g2, 0.0))))
    o_ref[...] = out


def _router(x, wh, wl, *, ne):
    m, d = x.shape
    bm = _tile(m, 512)
    return pl.pallas_call(
        functools.partial(_router_kernel, ne=ne),
        grid=(m // bm,),
        in_specs=[
            pl.BlockSpec((bm, d), lambda i: (i, 0)),
            pl.BlockSpec(wh.shape, lambda i: (0, 0)),
            pl.BlockSpec(wl.shape, lambda i: (0, 0)),
        ],
        out_specs=pl.BlockSpec((bm, V7X_LANES), lambda i: (i, 0)),
        out_shape=jax.ShapeDtypeStruct((m, V7X_LANES), F32),
        compiler_params=_cparams("parallel"),
        name="router",
    )(x, wh, wl)


DMA_ISSUE_UNROLL = 8


def _row_copy(src, src_row, dst, dst_row, sem):
    return pltpu.make_async_copy(src.at[pl.ds(src_row, 1)], dst.at[pl.ds(dst_row, 1)], sem)


def _dispatch_kernel(pos_ref, x_ref, init_hbm, xs_hbm, sem, *, bt):
    del init_hbm
    base = pl.program_id(0) * bt

    def issue(r, carry):
        for k in range(TOP_K):
            _row_copy(x_ref, r, xs_hbm, pos_ref[(base + r) * TOP_K + k], sem).start()
        return carry

    def drain(r, carry):
        for k in range(TOP_K):
            _row_copy(x_ref, 0, xs_hbm, 0, sem).wait()
        return carry

    lax.fori_loop(0, bt, issue, 0, unroll=DMA_ISSUE_UNROLL)
    lax.fori_loop(0, bt, drain, 0, unroll=DMA_ISSUE_UNROLL)


def _dispatch(x, pos, init):
    m, d = x.shape
    rows = init.shape[0]
    bt = _tile(m, 512)
    return pl.pallas_call(
        functools.partial(_dispatch_kernel, bt=bt),
        grid_spec=pltpu.PrefetchScalarGridSpec(
            num_scalar_prefetch=1,
            grid=(m // bt,),
            in_specs=[pl.BlockSpec((bt, d), lambda i, p: (i, 0)), pl.BlockSpec(memory_space=pl.ANY)],
            out_specs=pl.BlockSpec(memory_space=pl.ANY),
            scratch_shapes=[pltpu.SemaphoreType.DMA(())],
        ),
        out_shape=jax.ShapeDtypeStruct((rows, d), x.dtype),
        input_output_aliases={2: 0},
        compiler_params=_cparams("arbitrary"),
        name="moe_dispatch",
    )(pos, x, init)


def _combine_kernel(pos_ref, ys_hbm, x_ref, rt_ref, g_ref, b_ref, o_ref, buf, sem, *, bt, alpha):
    base = pl.program_id(0) * bt

    def issue(r, carry):
        for k in range(TOP_K):
            _row_copy(ys_hbm, pos_ref[(base + r) * TOP_K + k], buf.at[k], r, sem).start()
        return carry

    def drain(r, carry):
        for k in range(TOP_K):
            _row_copy(ys_hbm, 0, buf.at[k], 0, sem).wait()
        return carry

    lax.fori_loop(0, bt, issue, 0, unroll=DMA_ISSUE_UNROLL)
    lax.fori_loop(0, bt, drain, 0, unroll=DMA_ISSUE_UNROLL)
    rt = rt_ref[...]
    y = buf[0] * rt[:, TOP_K:TOP_K + 1]
    for k in range(1, TOP_K):
        y = y + buf[k] * rt[:, TOP_K + k:TOP_K + k + 1]
    o_ref[...] = _layer_norm(alpha * x_ref[...] + y, g_ref[...], b_ref[...])


def _combine(ys, pos, x, route, g, b, *, alpha):
    m, d = x.shape
    bt = _tile(m, 512)
    row = lambda i, p: (i, 0)
    full = lambda i, p: (0, 0)
    return pl.pallas_call(
        functools.partial(_combine_kernel, bt=bt, alpha=alpha),
        grid_spec=pltpu.PrefetchScalarGridSpec(
            num_scalar_prefetch=1,
            grid=(m // bt,),
            in_specs=[
                pl.BlockSpec(memory_space=pl.ANY),
                pl.BlockSpec((bt, d), row),
                pl.BlockSpec((bt, V7X_LANES), row),
                pl.BlockSpec((1, d), full),
                pl.BlockSpec((1, d), full),
            ],
            out_specs=pl.BlockSpec((bt, d), row),
            scratch_shapes=[pltpu.VMEM((TOP_K, bt, d), F32), pltpu.SemaphoreType.DMA(())],
        ),
        out_shape=jax.ShapeDtypeStruct((m, d), F32),
        compiler_params=_cparams("arbitrary"),
        name="moe_combine",
    )(pos, ys, x, route, g, b)


def _moe_plan(idx, ne, bm, nt):
    e_flat = idx.reshape(-1)
    onehot = (e_flat[None, :] == jnp.arange(ne, dtype=jnp.int32)[:, None]).astype(jnp.int32)
    csum = jnp.cumsum(onehot, axis=1)
    rank = jnp.sum(onehot * csum, axis=0) - 1
    counts = csum[:, -1]
    tiles_e = (counts + bm - 1) // bm
    tile_end = jnp.cumsum(tiles_e)
    tile_start = tile_end - tiles_e
    pos = (tile_start[e_flat] * bm + rank).astype(jnp.int32)
    t = jnp.arange(nt, dtype=jnp.int32)
    te = jnp.minimum(jnp.sum((t[:, None] >= tile_end[None, :]).astype(jnp.int32), axis=1), ne - 1)
    tv = jnp.clip(counts[te] - (t - tile_start[te]) * bm, 0, bm).astype(jnp.int32)
    return pos, te, tv


FFN_ROW_TILE = 1024
LN_ROW_CHUNK = 128


def _moe_layer(xs, wr_hi, wr_lo, wg, wu, wd, layer, g, b, *, ne, alpha):
    d = xs[0].shape[1]
    routes = [_router(x, wr_hi, wr_lo, ne=ne) for x in xs]
    idx = jnp.concatenate([r[:, :TOP_K] for r in routes], axis=0).astype(jnp.int32)
    bm = FFN_ROW_TILE
    nt = (idx.shape[0] * TOP_K) // bm + ne
    pos, te, tv = _moe_plan(idx, ne, bm, nt)
    bounds = [0]
    for x in xs:
        bounds.append(bounds[-1] + x.shape[0] * TOP_K)
    slots = [pos[lo:hi] for lo, hi in zip(bounds[:-1], bounds[1:])]
    sorted_rows = jnp.zeros((nt * bm, d), F32)
    for x, p in zip(xs, slots):
        sorted_rows = _dispatch(x, p, sorted_rows)
    ys = _ffn_call(sorted_rows, wg, wu, wd, layer, te, tv, bm=bm)
    return [_combine(ys, p, x, r, g, b, alpha=alpha) for x, p, r in zip(xs, slots, routes)]


def _dense_ffn_layer(x, wg, wu, wd, layer, g, b, *, alpha):
    m = x.shape[0]
    bm = _tile(m, FFN_ROW_TILE)
    nt = m // bm
    te = jnp.zeros((nt,), jnp.int32)
    tv = jnp.full((nt,), bm, jnp.int32)
    return _ffn_call(x, wg[:, None], wu[:, None], wd[:, None], layer, te, tv,
                     bm=bm, ln=(g, b), alpha=alpha)


def _rope_tables(pos, rope, scale):
    inv = ROPE_BASE ** (-jnp.arange(0, rope, 2, dtype=F32) / rope)
    ang = pos.astype(F32)[:, None] * inv[None, :]
    cos, sin = jnp.cos(ang), jnp.sin(ang)
    t1 = jnp.concatenate([cos, cos, sin, sin], axis=1)
    ones = jnp.ones((pos.shape[0], V7X_LANES), F32)
    t2 = jnp.concatenate([ones, t1], axis=1) * scale
    return t1, t2


def _swap_halves(w):
    half = w.shape[-1] // 2
    return jnp.concatenate([-w[..., half:], w[..., :half]], axis=-1)


def _prep_mla(w_dq, w_uq, w_dkv, w_uk, w_uv, w_o, *, kl, heads, nope, rope):
    ql = w_dq.shape[1]
    kr = w_dkv[:, kl:]
    w1 = jnp.concatenate([w_dq, w_dkv[:, :kl], kr, _swap_halves(kr)], axis=1).astype(BF16)
    uq = w_uq.reshape(ql, heads, nope + rope)
    qr = uq[..., nope:]
    w2 = jnp.concatenate([uq[..., :nope], qr, _swap_halves(qr)], axis=-1)
    w2 = w2.reshape(ql, heads * (nope + 2 * rope)).astype(BF16)
    wk = w_uk.reshape(kl, -1).astype(BF16)
    wv = w_uv.reshape(kl, -1).astype(BF16)
    return w1, w2, wk, wv, w_o.astype(BF16)


def kernel(x_prompt, x_sample, cache_ckv, cache_krope, state_pool, mla_w_dq, mla_q_norm, mla_w_uq, mla_w_dkv, mla_kv_norm, mla_w_uk, mla_w_uv, mla_w_o, pool_w, pool_scale, ffn_w_gate, ffn_w_up, ffn_w_down, moe_w_router, moe_w_gate, moe_w_up, moe_w_down, ln_mix_g, ln_mix_b, ln_ffn_g, ln_ffn_b):
    bp, sp, d = x_prompt.shape
    bs, ss, _ = x_sample.shape
    past = cache_ckv.shape[2]
    depth = ln_mix_g.shape[0]
    kl, heads, nope = mla_w_uk.shape[1:]
    vh = mla_w_uv.shape[3]
    ql = mla_w_dq.shape[2]
    rope = mla_w_dkv.shape[2] - kl
    hd = nope + 2 * rope
    ne = moe_w_router.shape[2]
    alpha = (2.0 * depth) ** 0.25
    scale = math.log2(math.e) / math.sqrt(nope + rope)
    assert bp == 1 and heads % 2 == 0 and 2 * rope == V7X_LANES and nope == V7X_LANES
    assert past % CHUNK == 0 and state_pool.shape[2] == POOL_HALO - 1

    t1_p, t2_p = _rope_tables(jnp.arange(sp, dtype=jnp.int32), rope, scale)
    t1_s, t2_s = _rope_tables(jnp.tile(past + jnp.arange(ss, dtype=jnp.int32), bs), rope, scale)

    xp = x_prompt.reshape(sp, d)
    xs = x_sample.reshape(bs * ss, d)
    row = lambda v: v.reshape(1, -1)
    ckv_p, kr_p, pool_p, ckv_s, kr_s, pool_s = [], [], [], [], [], []
    blk = _tile(sp, 512)

    for i in range(depth):
        j = i // 2
        if i % 2 == 0:
            w1, w2, wk, wv, wo = _prep_mla(mla_w_dq[j], mla_w_uq[j], mla_w_dkv[j], mla_w_uk[j],
                                           mla_w_uv[j], mla_w_o[j], kl=kl, heads=heads, nope=nope, rope=rope)
            dims = dict(ql=ql, kl=kl, rope=rope, heads=heads, hd=hd)
            up = dict(kl=kl, heads=heads, nope=nope, vh=vh, hd=hd)
            qn, kvn = row(mla_q_norm[j]), row(mla_kv_norm[j])
            lg, lb = row(ln_mix_g[i]), row(ln_mix_b[i])
            q, ckv, kr, ckr = _mla_proj(xp, w1, w2, qn, kvn, t1_p, t2_p, **dims)
            k, v = _kv_up(ckr, wk, wv, **up)
            o = _attn_prompt(q, k, v, blk=blk)
            xp = _proj_res_ln(o, wo, xp, lg, lb, alpha=alpha)
            ckv_p.append(ckv.reshape(bp, sp, kl))
            kr_p.append(kr.reshape(bp, sp, rope))
            q, ckv, kr, ckr = _mla_proj(xs, w1, w2, qn, kvn, t1_s, t2_s, **dims)
            kn, vn = _kv_up(ckr, wk, wv, **up)
            ckr_c = jnp.concatenate([cache_ckv[j], cache_krope[j], cache_krope[j]], axis=-1)
            kc, vc = _kv_up(ckr_c.reshape(bs * past, kl + 2 * rope).astype(BF16), wk, wv, **up)
            o = _attn_sample(q, kc, vc, kn, vn, batch=bs, past=past)
            xs = _proj_res_ln(o, wo, xs, lg, lb, alpha=alpha)
            ckv_s.append(ckv.reshape(bs, ss, kl))
            kr_s.append(kr.reshape(bs, ss, rope))
        else:
            pw = pool_w[j].astype(BF16)
            sc = row(pool_scale[j])
            lg, lb = row(ln_mix_g[i]), row(ln_mix_b[i])
            xp3 = xp.reshape(bp, sp, d)
            xs3 = xs.reshape(bs, ss, d)
            pool_p.append(xp3[:, sp - (POOL_HALO - 1):, :])
            pool_s.append(jnp.concatenate([state_pool[j], xs3], axis=1)[:, ss:, :])
            hist_p = jnp.zeros((bp, POOL_HALO, d), F32)
            hist_s = jnp.concatenate([jnp.zeros((bs, 1, d), F32), state_pool[j]], axis=1)
            xp = _pool_layer(xp3, hist_p, pw, sc, lg, lb, pos0=0, alpha=alpha).reshape(sp, d)
            xs = _pool_layer(xs3, hist_s, pw, sc, lg, lb, pos0=past, alpha=alpha).reshape(bs * ss, d)

        fg, fb = row(ln_ffn_g[i]), row(ln_ffn_b[i])
        if i % 2 == 0:
            xp = _dense_ffn_layer(xp, ffn_w_gate, ffn_w_up, ffn_w_down, j, fg, fb, alpha=alpha)
            xs = _dense_ffn_layer(xs, ffn_w_gate, ffn_w_up, ffn_w_down, j, fg, fb, alpha=alpha)
        else:
            wr = jnp.pad(moe_w_router[j], ((0, 0), (0, V7X_LANES - ne)))
            wr_hi = wr.astype(BF16)
            wr_lo = (wr - wr_hi.astype(F32)).astype(BF16)
            xp, xs = _moe_layer([xp, xs], wr_hi, wr_lo, moe_w_gate, moe_w_up, moe_w_down, j, fg, fb,
                                ne=ne, alpha=alpha)

    return (xp.reshape(bp, sp, d), xs.reshape(bs, ss, d),
            jnp.stack(ckv_p), jnp.stack(kr_p), jnp.stack(pool_p),
            jnp.stack(ckv_s), jnp.stack(kr_s), jnp.stack(pool_s))
```

```python
import functools
import math

import jax
import jax.numpy as jnp
from jax import lax
from jax.experimental import pallas as pl
from jax.experimental.pallas import tpu as pltpu

BF16 = jnp.bfloat16
F32 = jnp.float32

CHUNK = 64
POOL_WINDOWS = (2, 4, 8, 16)
POOL_HALO = 16
ROPE_BASE = 10000.0
LN_EPS = 1e-5
RMS_EPS = 1e-6
TOP_K = 2

V7X_LANES = 128
V7X_VMEM_BUDGET = 56 * 1024 * 1024


def _cparams(*sem):
    return pltpu.CompilerParams(dimension_semantics=sem, vmem_limit_bytes=V7X_VMEM_BUDGET)


def _tile(n, pref):
    if n <= pref:
        return n
    t = pref
    while n % t:
        t //= 2
    assert t >= 8, (n, pref)
    return t


def _layer_norm(z, g, b):
    mu = jnp.mean(z, axis=-1, keepdims=True)
    zc = z - mu
    var = jnp.mean(zc * zc, axis=-1, keepdims=True)
    return zc * lax.rsqrt(var + LN_EPS) * g + b


def _rms(c, g):
    return c * lax.rsqrt(jnp.mean(c * c, axis=-1, keepdims=True) + RMS_EPS) * g


def _mla_proj_kernel(x_ref, w1_ref, w2_ref, qn_ref, kvn_ref, t1_ref, t2_ref,
                     q_ref, ckv_ref, kr_ref, ckr_ref, *, ql, kl, rope, heads, hd):
    xb = x_ref[...].astype(BF16)
    r = jnp.dot(xb, w1_ref[...], preferred_element_type=F32)
    cq = _rms(r[:, :ql], qn_ref[...])
    ckv = _rms(r[:, ql:ql + kl], kvn_ref[...])
    t = r[:, ql + kl:] * t1_ref[...]
    kk = t + pltpu.roll(t, rope, axis=1)
    ckv_ref[...] = ckv
    kr_ref[...] = kk[:, :rope]
    ckr_ref[:, :kl] = ckv.astype(BF16)
    ckr_ref[:, kl:] = kk.astype(BF16)
    cqb = cq.astype(BF16)
    t2 = t2_ref[...]
    for h in range(heads):
        qh = jnp.dot(cqb, w2_ref[:, h * hd:(h + 1) * hd], preferred_element_type=F32)
        q_ref[h] = (qh * t2).astype(BF16)


def _mla_proj(x, w1, w2, qn, kvn, t1, t2, *, ql, kl, rope, heads, hd):
    m, d = x.shape
    bm = _tile(m, 512)
    kern = functools.partial(_mla_proj_kernel, ql=ql, kl=kl, rope=rope, heads=heads, hd=hd)
    row = lambda i: (i, 0)
    full = lambda i: (0, 0)
    return pl.pallas_call(
        kern,
        grid=(m // bm,),
        in_specs=[
            pl.BlockSpec((bm, d), row),
            pl.BlockSpec(w1.shape, full),
            pl.BlockSpec(w2.shape, full),
            pl.BlockSpec((1, ql), full),
            pl.BlockSpec((1, kl), full),
            pl.BlockSpec((bm, 2 * rope), row),
            pl.BlockSpec((bm, hd), row),
        ],
        out_specs=[
            pl.BlockSpec((heads, bm, hd), lambda i: (0, i, 0)),
            pl.BlockSpec((bm, kl), row),
            pl.BlockSpec((bm, rope), row),
            pl.BlockSpec((bm, kl + 2 * rope), row),
        ],
        out_shape=[
            jax.ShapeDtypeStruct((heads, m, hd), BF16),
            jax.ShapeDtypeStruct((m, kl), F32),
            jax.ShapeDtypeStruct((m, rope), F32),
            jax.ShapeDtypeStruct((m, kl + 2 * rope), BF16),
        ],
        compiler_params=_cparams("parallel"),
        name="mla_proj",
    )(x, w1, w2, qn, kvn, t1, t2)


def _kv_up_kernel(ckr_ref, wk_ref, wv_ref, k_ref, v_ref, *, kl, heads, nope, vh):
    c = ckr_ref[:, :kl]
    kk = ckr_ref[:, kl:]
    for p in range(heads // 2):
        kn = jnp.dot(c, wk_ref[:, 2 * p * nope:(2 * p + 2) * nope], preferred_element_type=F32)
        vv = jnp.dot(c, wv_ref[:, 2 * p * vh:(2 * p + 2) * vh], preferred_element_type=F32)
        for s in range(2):
            h = 2 * p + s
            k_ref[h, :, :nope] = kn[:, s * nope:(s + 1) * nope].astype(BF16)
            k_ref[h, :, nope:] = kk
            v_ref[h] = vv[:, s * vh:(s + 1) * vh].astype(BF16)


def _kv_up(ckr, wk, wv, *, kl, heads, nope, vh, hd):
    m = ckr.shape[0]
    bm = _tile(m, 512)
    kern = functools.partial(_kv_up_kernel, kl=kl, heads=heads, nope=nope, vh=vh)
    return pl.pallas_call(
        kern,
        grid=(m // bm,),
        in_specs=[
            pl.BlockSpec((bm, ckr.shape[1]), lambda i: (i, 0)),
            pl.BlockSpec(wk.shape, lambda i: (0, 0)),
            pl.BlockSpec(wv.shape, lambda i: (0, 0)),
        ],
        out_specs=[
            pl.BlockSpec((heads, bm, hd), lambda i: (0, i, 0)),
            pl.BlockSpec((heads, bm, vh), lambda i: (0, i, 0)),
        ],
        out_shape=[
            jax.ShapeDtypeStruct((heads, m, hd), BF16),
            jax.ShapeDtypeStruct((heads, m, vh), BF16),
        ],
        compiler_params=_cparams("parallel"),
        name="kv_up",
    )(ckr, wk, wv)


def _attn_prompt_kernel(q_ref, k_ref, v_ref, o_ref, m_scr, l_scr, acc_scr, *, blk, hp, qs, vh):
    i = pl.program_id(1)
    m_scr[...] = jnp.full(m_scr.shape, -jnp.inf, F32)
    l_scr[...] = jnp.zeros(l_scr.shape, F32)
    acc_scr[...] = jnp.zeros(acc_scr.shape, F32)

    def update(a, u, k, v, diagonal):
        rows = slice(u * blk, (u + 1) * blk)
        s = lax.dot_general(q_ref[a, rows, :], k, (((1,), (1,)), ((), ())), preferred_element_type=F32)
        if diagonal:
            qc = lax.broadcasted_iota(jnp.int32, s.shape, 0) // CHUNK
            kc = lax.broadcasted_iota(jnp.int32, s.shape, 1) // CHUNK
            s = jnp.where(kc <= qc, s, -jnp.inf)
        cols = [s[:, c * V7X_LANES:(c + 1) * V7X_LANES] for c in range(s.shape[1] // V7X_LANES)]
        mx = cols[0]
        for c in cols[1:]:
            mx = jnp.maximum(mx, c)
        m_prev = m_scr[a, rows, :]
        m_new = jnp.maximum(m_prev, jnp.max(mx, axis=-1, keepdims=True))
        alpha = jnp.exp2(m_prev - m_new)
        ps = [jnp.exp2(c - m_new) for c in cols]
        lsum = ps[0]
        for pc in ps[1:]:
            lsum = lsum + pc
        l_scr[a, rows, :] = alpha * l_scr[a, rows, :] + lsum
        p = jnp.concatenate([pc.astype(BF16) for pc in ps], axis=1)
        acc_scr[a, rows, :] = alpha * acc_scr[a, rows, :] + jnp.dot(p, v, preferred_element_type=F32)
        m_scr[a, rows, :] = m_new

    def key_block(start, width, subs, diagonal):
        for a in range(hp):
            k = k_ref[a, pl.ds(start, width), :]
            v = v_ref[a, pl.ds(start, width), :]
            for u in subs:
                update(a, u, k, v, diagonal)

    def fully_visible(jj, carry):
        key_block(pl.multiple_of(jj * (blk * qs), blk * qs), blk * qs, range(qs), False)
        return carry

    lax.fori_loop(0, i, fully_visible, 0)
    own = pl.multiple_of(i * (blk * qs), blk * qs)
    for u in range(qs):
        if u:
            key_block(own, u * blk, [u], False)
        key_block(pl.multiple_of(own + u * blk, blk), blk, [u], True)
    for a in range(hp):
        l = jnp.sum(l_scr[a], axis=-1, keepdims=True)
        o_ref[:, a * vh:(a + 1) * vh] = (acc_scr[a] / l).astype(o_ref.dtype)


def _attn_prompt(q, k, v, *, blk, hp=2, qs=4):
    heads, s, hd = q.shape
    vh = v.shape[2]
    bq = blk * qs
    assert heads % hp == 0 and vh == V7X_LANES and s % bq == 0
    kern = functools.partial(_attn_prompt_kernel, blk=blk, hp=hp, qs=qs, vh=vh)
    once = pl.Buffered(1)
    return pl.pallas_call(
        kern,
        grid=(heads // hp, s // bq),
        in_specs=[
            pl.BlockSpec((hp, bq, hd), lambda h, i: (h, i, 0)),
            pl.BlockSpec((hp, s, hd), lambda h, i: (h, 0, 0), pipeline_mode=once),
            pl.BlockSpec((hp, s, vh), lambda h, i: (h, 0, 0), pipeline_mode=once),
        ],
        out_specs=pl.BlockSpec((bq, hp * vh), lambda h, i: (i, h)),
        out_shape=jax.ShapeDtypeStruct((s, heads * vh), BF16),
        scratch_shapes=[
            pltpu.VMEM((hp, bq, V7X_LANES), F32),
            pltpu.VMEM((hp, bq, V7X_LANES), F32),
            pltpu.VMEM((hp, bq, vh), F32),
        ],
        compiler_params=_cparams("parallel", "arbitrary"),
        name="attn_prompt",
    )(q, k, v)


def _attn_sample_kernel(q_ref, kc_ref, vc_ref, kn_ref, vn_ref, o_ref, *, past, hp, vh):
    dn = (((1,), (1,)), ((), ()))
    for a in range(hp):
        q = q_ref[a]
        s1 = lax.dot_general(q, kc_ref[a], dn, preferred_element_type=F32)
        s2 = lax.dot_general(q, kn_ref[a], dn, preferred_element_type=F32)
        qc1 = (past + lax.broadcasted_iota(jnp.int32, s1.shape, 0)) // CHUNK
        kc1 = lax.broadcasted_iota(jnp.int32, s1.shape, 1) // CHUNK
        s1 = jnp.where(kc1 <= qc1, s1, -jnp.inf)
        qc2 = (past + lax.broadcasted_iota(jnp.int32, s2.shape, 0)) // CHUNK
        kc2 = (past + lax.broadcasted_iota(jnp.int32, s2.shape, 1)) // CHUNK
        s2 = jnp.where(kc2 <= qc2, s2, -jnp.inf)
        m = jnp.maximum(jnp.max(s1, axis=-1, keepdims=True), jnp.max(s2, axis=-1, keepdims=True))
        p1 = jnp.exp2(s1 - m)
        p2 = jnp.exp2(s2 - m)
        l = jnp.sum(p1, axis=-1, keepdims=True) + jnp.sum(p2, axis=-1, keepdims=True)
        o = (jnp.dot(p1.astype(BF16), vc_ref[a], preferred_element_type=F32)
             + jnp.dot(p2.astype(BF16), vn_ref[a], preferred_element_type=F32))
        o_ref[:, a * vh:(a + 1) * vh] = (o / l).astype(o_ref.dtype)


def _attn_sample(q, kc, vc, kn, vn, *, batch, past):
    heads, m, hd = q.shape
    sq = m // batch
    vh = vc.shape[2]
    hp = math.gcd(heads, 4)
    kern = functools.partial(_attn_sample_kernel, past=past, hp=hp, vh=vh)
    return pl.pallas_call(
        kern,
        grid=(batch, heads // hp),
        in_specs=[
            pl.BlockSpec((hp, sq, hd), lambda b, h: (h, b, 0)),
            pl.BlockSpec((hp, past, hd), lambda b, h: (h, b, 0)),
            pl.BlockSpec((hp, past, vh), lambda b, h: (h, b, 0)),
            pl.BlockSpec((hp, sq, hd), lambda b, h: (h, b, 0)),
            pl.BlockSpec((hp, sq, vh), lambda b, h: (h, b, 0)),
        ],
        out_specs=pl.BlockSpec((sq, hp * vh), lambda b, h: (b, h)),
        out_shape=jax.ShapeDtypeStruct((m, heads * vh), BF16),
        compiler_params=_cparams("parallel", "parallel"),
        name="attn_sample",
    )(q, kc, vc, kn, vn)


def _proj_res_ln_kernel(a_ref, w_ref, x_ref, g_ref, b_ref, o_ref, *, alpha):
    y = jnp.dot(a_ref[...], w_ref[...], preferred_element_type=F32)
    o_ref[...] = _layer_norm(alpha * x_ref[...] + y, g_ref[...], b_ref[...])


def _proj_res_ln(a, w, x, g, b, *, alpha):
    m, k = a.shape
    d = w.shape[1]
    bm = _tile(m, 512)
    row = lambda i: (i, 0)
    full = lambda i: (0, 0)
    return pl.pallas_call(
        functools.partial(_proj_res_ln_kernel, alpha=alpha),
        grid=(m // bm,),
        in_specs=[
            pl.BlockSpec((bm, k), row),
            pl.BlockSpec((k, d), full),
            pl.BlockSpec((bm, d), row),
            pl.BlockSpec((1, d), full),
            pl.BlockSpec((1, d), full),
        ],
        out_specs=pl.BlockSpec((bm, d), row),
        out_shape=jax.ShapeDtypeStruct((m, d), F32),
        compiler_params=_cparams("parallel"),
        name="proj_res_ln",
    )(a, w, x, g, b)


def _pool_kernel(x_ref, prev_ref, hist_ref, w_ref, sc_ref, g_ref, b_ref, o_ref, buf, y_scr,
                 *, bm, gw, pos0, alpha):
    i = pl.program_id(1)
    x = x_ref[...]
    halo = jnp.where(i == 0, hist_ref[...], prev_ref[...])
    buf[0:POOL_HALO, :] = halo
    buf[POOL_HALO:POOL_HALO + bm, :] = x
    pos = pos0 + i * bm + lax.broadcasted_iota(jnp.int32, (bm, 1), 0)
    for gi, w in enumerate(POOL_WINDOWS):
        lo, hi = gi * gw, (gi + 1) * gw
        tot = x[:, lo:hi]
        for j in range(1, w):
            tot = tot + buf[POOL_HALO - j:POOL_HALO - j + bm, lo:hi]
        cnt = jnp.minimum(pos + 1, w).astype(F32)
        dlt = (tot / cnt - x[:, lo:hi]).astype(BF16)
        y_scr[:, lo:hi] = jnp.dot(dlt, w_ref[gi], preferred_element_type=F32)
    y = y_scr[...] * sc_ref[...]
    o_ref[...] = _layer_norm(alpha * x + y, g_ref[...], b_ref[...])


def _pool_layer(x, hist, w, sc, g, b, *, pos0, alpha):
    bsz, s, d = x.shape
    gw = d // len(POOL_WINDOWS)
    bm = _tile(s, 512)
    assert bm % POOL_HALO == 0
    hb = bm // POOL_HALO
    kern = functools.partial(_pool_kernel, bm=bm, gw=gw, pos0=pos0, alpha=alpha)
    vec = lambda bb, i: (0, 0)
    return pl.pallas_call(
        kern,
        grid=(bsz, s // bm),
        in_specs=[
            pl.BlockSpec((None, bm, d), lambda bb, i: (bb, i, 0)),
            pl.BlockSpec((None, POOL_HALO, d), lambda bb, i: (bb, jnp.maximum(i * hb - 1, 0), 0)),
            pl.BlockSpec((None, POOL_HALO, d), lambda bb, i: (bb, 0, 0)),
            pl.BlockSpec(w.shape, lambda bb, i: (0, 0, 0)),
            pl.BlockSpec((1, d), vec),
            pl.BlockSpec((1, d), vec),
            pl.BlockSpec((1, d), vec),
        ],
        out_specs=pl.BlockSpec((None, bm, d), lambda bb, i: (bb, i, 0)),
        out_shape=jax.ShapeDtypeStruct((bsz, s, d), F32),
        scratch_shapes=[
            pltpu.VMEM((bm + POOL_HALO, d), F32),
            pltpu.VMEM((bm, d), F32),
        ],
        compiler_params=_cparams("parallel", "arbitrary"),
        name="pool_mix",
    )(x, x, hist, w, sc, g, b)


def _swiglu_tile(tv_ref, x_ref, wg_ref, wu_ref, wd_ref, o_ref, xb_scr, wgb, wub, wdb, *, bm, sb):
    i = pl.program_id(0)
    f = pl.program_id(1)
    valid = tv_ref[i]

    @pl.when(f == 0)
    def _():
        o_ref[...] = jnp.zeros(o_ref.shape, F32)
        xb_scr[...] = x_ref[...].astype(BF16)

    def rows(lo, hi, wg, wu, wd):
        xs = xb_scr[lo:hi, :]
        gte = jnp.dot(xs, wg, preferred_element_type=F32)
        up = jnp.dot(xs, wu, preferred_element_type=F32)
        hid = (gte * jax.nn.sigmoid(gte) * up).astype(BF16)
        o_ref[lo:hi, :] += jnp.dot(hid, wd, preferred_element_type=F32)

    full = valid > bm - sb

    @pl.when(full)
    def _():
        rows(0, bm, wg_ref[...].astype(BF16), wu_ref[...].astype(BF16), wd_ref[...].astype(BF16))

    @pl.when((valid > 0) & jnp.logical_not(full))
    def _():
        wgb[...] = wg_ref[...].astype(BF16)
        wub[...] = wu_ref[...].astype(BF16)
        wdb[...] = wd_ref[...].astype(BF16)

    for s in range(bm // sb - 1):
        @pl.when((s * sb < valid) & jnp.logical_not(full))
        def _():
            rows(s * sb, (s + 1) * sb, wgb[...], wub[...], wdb[...])


def _ffn_dense_kernel(te_ref, tv_ref, x_ref, wg_ref, wu_ref, wd_ref, g_ref, b_ref, o_ref,
                      xb_scr, wgb, wub, wdb, *, bm, sb, alpha):
    del te_ref
    _swiglu_tile(tv_ref, x_ref, wg_ref, wu_ref, wd_ref, o_ref, xb_scr, wgb, wub, wdb, bm=bm, sb=sb)

    @pl.when(pl.program_id(1) == pl.num_programs(1) - 1)
    def _():
        rc = _tile(bm, LN_ROW_CHUNK)

        def norm_rows(c, carry):
            r = pl.ds(pl.multiple_of(c * rc, rc), rc)
            o_ref[r, :] = _layer_norm(alpha * x_ref[r, :] + o_ref[r, :], g_ref[...], b_ref[...])
            return carry

        lax.fori_loop(0, bm // rc, norm_rows, 0)


def _ffn_expert_kernel(te_ref, tv_ref, x_ref, wg_ref, wu_ref, wd_ref, o_ref,
                       xb_scr, wgb, wub, wdb, *, bm, sb):
    del te_ref
    _swiglu_tile(tv_ref, x_ref, wg_ref, wu_ref, wd_ref, o_ref, xb_scr, wgb, wub, wdb, bm=bm, sb=sb)


def _ffn_call(x, wg, wu, wd, layer, tile_expert, tile_valid, *, bm, ln=None, alpha=None):
    r, d = x.shape
    nf_total = wg.shape[3]
    bf = _tile(nf_total, 256)
    nf = nf_total // bf
    sb = _tile(bm, 256)
    nt = r // bm

    def chunk(i, f, te, tv):
        return jnp.where(tv[i] > 0, f, nf - 1)

    in_specs = [
        pl.BlockSpec((bm, d), lambda i, f, te, tv: (i, 0)),
        pl.BlockSpec((None, None, d, bf), lambda i, f, te, tv: (layer, te[i], 0, chunk(i, f, te, tv))),
        pl.BlockSpec((None, None, d, bf), lambda i, f, te, tv: (layer, te[i], 0, chunk(i, f, te, tv))),
        pl.BlockSpec((None, None, bf, d), lambda i, f, te, tv: (layer, te[i], chunk(i, f, te, tv), 0)),
    ]
    args = [x, wg, wu, wd]
    if ln is not None:
        in_specs += [pl.BlockSpec((1, d), lambda i, f, te, tv: (0, 0))] * 2
        args += list(ln)
        kern = functools.partial(_ffn_dense_kernel, bm=bm, sb=sb, alpha=alpha)
    else:
        kern = functools.partial(_ffn_expert_kernel, bm=bm, sb=sb)
    return pl.pallas_call(
        kern,
        grid_spec=pltpu.PrefetchScalarGridSpec(
            num_scalar_prefetch=2,
            grid=(nt, nf),
            in_specs=in_specs,
            out_specs=pl.BlockSpec((bm, d), lambda i, f, te, tv: (i, 0)),
            scratch_shapes=[pltpu.VMEM((bm, d), BF16), pltpu.VMEM((d, bf), BF16),
                            pltpu.VMEM((d, bf), BF16), pltpu.VMEM((bf, d), BF16)],
        ),
        out_shape=jax.ShapeDtypeStruct((r, d), F32),
        compiler_params=_cparams("parallel", "arbitrary"),
        name="swiglu_dense" if ln is not None else "swiglu_experts",
    )(tile_expert, tile_valid, *args)


def _router_kernel(x_ref, wh_ref, wl_ref, o_ref, *, ne):
    x = x_ref[...]
    xh = x.astype(BF16)
    xl = (x - xh.astype(F32)).astype(BF16)
    wh = wh_ref[...]
    lg = (jnp.dot(xh, wh, preferred_element_type=F32)
          + (jnp.dot(xl, wh, preferred_element_type=F32)
             + jnp.dot(xh, wl_ref[...], preferred_element_type=F32)))
    lane = lax.broadcasted_iota(jnp.int32, lg.shape, 1)
    lg = jnp.where(lane < ne, lg, -jnp.inf)
    m1 = jnp.max(lg, axis=-1, keepdims=True)
    i1 = jnp.min(jnp.where(lg == m1, lane, V7X_LANES), axis=-1, keepdims=True)
    lg2 = jnp.where(lane == i1, -jnp.inf, lg)
    m2 = jnp.max(lg2, axis=-1, keepdims=True)
    i2 = jnp.min(jnp.where(lg2 == m2, lane, V7X_LANES), axis=-1, keepdims=True)
    e = jnp.exp(m2 - m1)
    den = 1.0 + e
    g1 = 1.0 / den
    g2 = e / den
    out = jnp.where(lane == 0, i1.astype(F32),
                    jnp.where(lane == 1, i2.astype(F32),
                              jnp.where(lane == 2, g1, jnp.where(lane == 3, g2, 0.0))))
    o_ref[...] = out


def _router(x, wh, wl, *, ne):
    m, d = x.shape
    bm = _tile(m, 512)
    return pl.pallas_call(
        functools.partial(_router_kernel, ne=ne),
        grid=(m // bm,),
        in_specs=[
            pl.BlockSpec((bm, d), lambda i: (i, 0)),
            pl.BlockSpec(wh.shape, lambda i: (0, 0)),
            pl.BlockSpec(wl.shape, lambda i: (0, 0)),
        ],
        out_specs=pl.BlockSpec((bm, V7X_LANES), lambda i: (i, 0)),
        out_shape=jax.ShapeDtypeStruct((m, V7X_LANES), F32),
        compiler_params=_cparams("parallel"),
        name="router",
    )(x, wh, wl)


DMA_ISSUE_UNROLL = 8


def _row_copy(src, src_row, dst, dst_row, sem):
    return pltpu.make_async_copy(src.at[pl.ds(src_row, 1)], dst.at[pl.ds(dst_row, 1)], sem)


def _dispatch_kernel(pos_ref, pad_ref, x_ref, xt_ref, out_hbm, zrow, sem, zsem,
                     *, bt, m_main, m_tail, n_pad, pad_step):
    i = pl.program_id(0)
    base = i * bt

    @pl.when(i == 0)
    def _():
        zrow[...] = jnp.zeros(zrow.shape, F32)

    def scatter(src, count, first_token, unroll):
        def issue(r, carry):
            for k in range(TOP_K):
                _row_copy(src, r, out_hbm, pos_ref[(first_token + r) * TOP_K + k], sem).start()
            return carry

        def drain(r, carry):
            for k in range(TOP_K):
                _row_copy(src, 0, out_hbm, 0, sem).wait()
            return carry

        lax.fori_loop(0, count, issue, 0, unroll=unroll)
        lax.fori_loop(0, count, drain, 0, unroll=unroll)

    scatter(x_ref, bt, base, DMA_ISSUE_UNROLL)

    lo = i * pad_step
    hi = jnp.minimum(lo + pad_step, n_pad)

    def fill(k, carry):
        _row_copy(zrow, 0, out_hbm, pad_ref[k], zsem).start()
        return carry

    def fill_done(k, carry):
        _row_copy(zrow, 0, out_hbm, 0, zsem).wait()
        return carry

    lax.fori_loop(lo, hi, fill, 0)
    lax.fori_loop(lo, hi, fill_done, 0)

    @pl.when(i == pl.num_programs(0) - 1)
    def _():
        scatter(xt_ref, m_tail, m_main, 1)


def _dispatch(x_main, x_tail, pos, pad_slots, rows):
    m, d = x_main.shape
    mt = x_tail.shape[0]
    bt = _tile(m, 512)
    steps = m // bt
    n_pad = pad_slots.shape[0]
    kern = functools.partial(_dispatch_kernel, bt=bt, m_main=m, m_tail=mt, n_pad=n_pad,
                             pad_step=pl.cdiv(n_pad, steps))
    return pl.pallas_call(
        kern,
        grid_spec=pltpu.PrefetchScalarGridSpec(
            num_scalar_prefetch=2,
            grid=(steps,),
            in_specs=[pl.BlockSpec((bt, d), lambda i, p, z: (i, 0)),
                      pl.BlockSpec((mt, d), lambda i, p, z: (0, 0))],
            out_specs=pl.BlockSpec(memory_space=pl.ANY),
            scratch_shapes=[pltpu.VMEM((8, d), F32), pltpu.SemaphoreType.DMA(()),
                            pltpu.SemaphoreType.DMA(())],
        ),
        out_shape=jax.ShapeDtypeStruct((rows, d), x_main.dtype),
        compiler_params=_cparams("arbitrary"),
        name="moe_dispatch",
    )(pos, pad_slots, x_main, x_tail)


def _combine_kernel(pos_ref, ys_hbm, x_ref, rt_ref, g_ref, b_ref, o_ref, buf, sem, *, bt, alpha):
    base = pl.program_id(0) * bt

    def issue(r, carry):
        for k in range(TOP_K):
            _row_copy(ys_hbm, pos_ref[(base + r) * TOP_K + k], buf.at[k], r, sem).start()
        return carry

    def drain(r, carry):
        for k in range(TOP_K):
            _row_copy(ys_hbm, 0, buf.at[k], 0, sem).wait()
        return carry

    lax.fori_loop(0, bt, issue, 0, unroll=DMA_ISSUE_UNROLL)
    lax.fori_loop(0, bt, drain, 0, unroll=DMA_ISSUE_UNROLL)
    rt = rt_ref[...]
    y = buf[0] * rt[:, TOP_K:TOP_K + 1]
    for k in range(1, TOP_K):
        y = y + buf[k] * rt[:, TOP_K + k:TOP_K + k + 1]
    o_ref[...] = _layer_norm(alpha * x_ref[...] + y, g_ref[...], b_ref[...])


def _combine(ys, pos, x, route, g, b, *, alpha):
    m, d = x.shape
    bt = _tile(m, 512)
    row = lambda i, p: (i, 0)
    full = lambda i, p: (0, 0)
    return pl.pallas_call(
        functools.partial(_combine_kernel, bt=bt, alpha=alpha),
        grid_spec=pltpu.PrefetchScalarGridSpec(
            num_scalar_prefetch=1,
            grid=(m // bt,),
            in_specs=[
                pl.BlockSpec(memory_space=pl.ANY),
                pl.BlockSpec((bt, d), row),
                pl.BlockSpec((bt, V7X_LANES), row),
                pl.BlockSpec((1, d), full),
                pl.BlockSpec((1, d), full),
            ],
            out_specs=pl.BlockSpec((bt, d), row),
            scratch_shapes=[pltpu.VMEM((TOP_K, bt, d), F32), pltpu.SemaphoreType.DMA(())],
        ),
        out_shape=jax.ShapeDtypeStruct((m, d), F32),
        compiler_params=_cparams("arbitrary"),
        name="moe_combine",
    )(pos, ys, x, route, g, b)


def _moe_plan(idx, ne, bm, nt):
    e_flat = idx.reshape(-1)
    onehot = (e_flat[None, :] == jnp.arange(ne, dtype=jnp.int32)[:, None]).astype(jnp.int32)
    csum = jnp.cumsum(onehot, axis=1)
    rank = jnp.sum(onehot * csum, axis=0) - 1
    counts = csum[:, -1]
    tiles_e = (counts + bm - 1) // bm
    tile_end = jnp.cumsum(tiles_e)
    tile_start = tile_end - tiles_e
    pos = (tile_start[e_flat] * bm + rank).astype(jnp.int32)
    t = jnp.arange(nt, dtype=jnp.int32)
    te = jnp.minimum(jnp.sum((t[:, None] >= tile_end[None, :]).astype(jnp.int32), axis=1), ne - 1)
    tv = jnp.clip(counts[te] - (t - tile_start[te]) * bm, 0, bm).astype(jnp.int32)
    seg_start = jnp.concatenate([tile_start * bm + counts, tile_end[-1:] * bm])
    seg_len = jnp.concatenate([tile_end * bm, jnp.full((1,), nt * bm, jnp.int32)]) - seg_start
    seg_end = jnp.cumsum(seg_len)
    k = jnp.arange(nt * bm - e_flat.shape[0], dtype=jnp.int32)
    seg = jnp.sum((k[:, None] >= seg_end[None, :]).astype(jnp.int32), axis=1)
    pad_slots = (seg_start[seg] + k - (seg_end[seg] - seg_len[seg])).astype(jnp.int32)
    return pos, pad_slots, te, tv


FFN_ROW_TILE = 1024
LN_ROW_CHUNK = 128


def _moe_layer(xs, wr_hi, wr_lo, wg, wu, wd, layer, g, b, *, ne, alpha):
    routes = [_router(x, wr_hi, wr_lo, ne=ne) for x in xs]
    idx = jnp.concatenate([r[:, :TOP_K] for r in routes], axis=0).astype(jnp.int32)
    bm = FFN_ROW_TILE
    nt = (idx.shape[0] * TOP_K) // bm + ne
    pos, pad_slots, te, tv = _moe_plan(idx, ne, bm, nt)
    bounds = [0]
    for x in xs:
        bounds.append(bounds[-1] + x.shape[0] * TOP_K)
    slots = [pos[lo:hi] for lo, hi in zip(bounds[:-1], bounds[1:])]
    x_main, x_tail = xs
    sorted_rows = _dispatch(x_main, x_tail, pos, pad_slots, nt * bm)
    ys = _ffn_call(sorted_rows, wg, wu, wd, layer, te, tv, bm=bm)
    return [_combine(ys, p, x, r, g, b, alpha=alpha) for x, p, r in zip(xs, slots, routes)]


def _dense_ffn_layer(x, wg, wu, wd, layer, g, b, *, alpha):
    m = x.shape[0]
    bm = _tile(m, FFN_ROW_TILE)
    nt = m // bm
    te = jnp.zeros((nt,), jnp.int32)
    tv = jnp.full((nt,), bm, jnp.int32)
    return _ffn_call(x, wg[:, None], wu[:, None], wd[:, None], layer, te, tv,
                     bm=bm, ln=(g, b), alpha=alpha)


def _rope_tables(pos, rope, scale):
    inv = ROPE_BASE ** (-jnp.arange(0, rope, 2, dtype=F32) / rope)
    ang = pos.astype(F32)[:, None] * inv[None, :]
    cos, sin = jnp.cos(ang), jnp.sin(ang)
    t1 = jnp.concatenate([cos, cos, sin, sin], axis=1)
    ones = jnp.ones((pos.shape[0], V7X_LANES), F32)
    t2 = jnp.concatenate([ones, t1], axis=1) * scale
    return t1, t2


def _swap_halves(w):
    half = w.shape[-1] // 2
    return jnp.concatenate([-w[..., half:], w[..., :half]], axis=-1)


def _prep_mla(w_dq, w_uq, w_dkv, w_uk, w_uv, w_o, *, kl, heads, nope, rope):
    ql = w_dq.shape[1]
    kr = w_dkv[:, kl:]
    w1 = jnp.concatenate([w_dq, w_dkv[:, :kl], kr, _swap_halves(kr)], axis=1).astype(BF16)
    uq = w_uq.reshape(ql, heads, nope + rope)
    qr = uq[..., nope:]
    w2 = jnp.concatenate([uq[..., :nope], qr, _swap_halves(qr)], axis=-1)
    w2 = w2.reshape(ql, heads * (nope + 2 * rope)).astype(BF16)
    wk = w_uk.reshape(kl, -1).astype(BF16)
    wv = w_uv.reshape(kl, -1).astype(BF16)
    return w1, w2, wk, wv, w_o.astype(BF16)


def kernel(x_prompt, x_sample, cache_ckv, cache_krope, state_pool, mla_w_dq, mla_q_norm, mla_w_uq, mla_w_dkv, mla_kv_norm, mla_w_uk, mla_w_uv, mla_w_o, pool_w, pool_scale, ffn_w_gate, ffn_w_up, ffn_w_down, moe_w_router, moe_w_gate, moe_w_up, moe_w_down, ln_mix_g, ln_mix_b, ln_ffn_g, ln_ffn_b):
    bp, sp, d = x_prompt.shape
    bs, ss, _ = x_sample.shape
    past = cache_ckv.shape[2]
    depth = ln_mix_g.shape[0]
    kl, heads, nope = mla_w_uk.shape[1:]
    vh = mla_w_uv.shape[3]
    ql = mla_w_dq.shape[2]
    rope = mla_w_dkv.shape[2] - kl
    hd = nope + 2 * rope
    ne = moe_w_router.shape[2]
    alpha = (2.0 * depth) ** 0.25
    scale = math.log2(math.e) / math.sqrt(nope + rope)
    assert bp == 1 and heads % 2 == 0 and 2 * rope == V7X_LANES and nope == V7X_LANES
    assert past % CHUNK == 0 and state_pool.shape[2] == POOL_HALO - 1

    t1_p, t2_p = _rope_tables(jnp.arange(sp, dtype=jnp.int32), rope, scale)
    t1_s, t2_s = _rope_tables(jnp.tile(past + jnp.arange(ss, dtype=jnp.int32), bs), rope, scale)

    xp = x_prompt.reshape(sp, d)
    xs = x_sample.reshape(bs * ss, d)
    row = lambda v: v.reshape(1, -1)
    ckv_p, kr_p, pool_p, ckv_s, kr_s, pool_s = [], [], [], [], [], []
    blk = _tile(sp, 512)

    for i in range(depth):
        j = i // 2
        if i % 2 == 0:
            w1, w2, wk, wv, wo = _prep_mla(mla_w_dq[j], mla_w_uq[j], mla_w_dkv[j], mla_w_uk[j],
                                           mla_w_uv[j], mla_w_o[j], kl=kl, heads=heads, nope=nope, rope=rope)
            dims = dict(ql=ql, kl=kl, rope=rope, heads=heads, hd=hd)
            up = dict(kl=kl, heads=heads, nope=nope, vh=vh, hd=hd)
            qn, kvn = row(mla_q_norm[j]), row(mla_kv_norm[j])
            lg, lb = row(ln_mix_g[i]), row(ln_mix_b[i])
            q, ckv, kr, ckr = _mla_proj(xp, w1, w2, qn, kvn, t1_p, t2_p, **dims)
            k, v = _kv_up(ckr, wk, wv, **up)
            o = _attn_prompt(q, k, v, blk=blk)
            xp = _proj_res_ln(o, wo, xp, lg, lb, alpha=alpha)
            ckv_p.append(ckv.reshape(bp, sp, kl))
            kr_p.append(kr.reshape(bp, sp, rope))
            q, ckv, kr, ckr = _mla_proj(xs, w1, w2, qn, kvn, t1_s, t2_s, **dims)
            kn, vn = _kv_up(ckr, wk, wv, **up)
            ckr_c = jnp.concatenate([cache_ckv[j], cache_krope[j], cache_krope[j]], axis=-1)
            kc, vc = _kv_up(ckr_c.reshape(bs * past, kl + 2 * rope).astype(BF16), wk, wv, **up)
            o = _attn_sample(q, kc, vc, kn, vn, batch=bs, past=past)
            xs = _proj_res_ln(o, wo, xs, lg, lb, alpha=alpha)
            ckv_s.append(ckv.reshape(bs, ss, kl))
            kr_s.append(kr.reshape(bs, ss, rope))
        else:
            pw = pool_w[j].astype(BF16)
            sc = row(pool_scale[j])
            lg, lb = row(ln_mix_g[i]), row(ln_mix_b[i])
            xp3 = xp.reshape(bp, sp, d)
            xs3 = xs.reshape(bs, ss, d)
            pool_p.append(xp3[:, sp - (POOL_HALO - 1):, :])
            pool_s.append(jnp.concatenate([state_pool[j], xs3], axis=1)[:, ss:, :])
            hist_p = jnp.zeros((bp, POOL_HALO, d), F32)
            hist_s = jnp.concatenate([jnp.zeros((bs, 1, d), F32), state_pool[j]], axis=1)
            xp = _pool_layer(xp3, hist_p, pw, sc, lg, lb, pos0=0, alpha=alpha).reshape(sp, d)
            xs = _pool_layer(xs3, hist_s, pw, sc, lg, lb, pos0=past, alpha=alpha).reshape(bs * ss, d)

        fg, fb = row(ln_ffn_g[i]), row(ln_ffn_b[i])
        if i % 2 == 0:
            xp = _dense_ffn_layer(xp, ffn_w_gate, ffn_w_up, ffn_w_down, j, fg, fb, alpha=alpha)
            xs = _dense_ffn_layer(xs, ffn_w_gate, ffn_w_up, ffn_w_down, j, fg, fb, alpha=alpha)
        else:
            wr = jnp.pad(moe_w_router[j], ((0, 0), (0, V7X_LANES - ne)))
            wr_hi = wr.astype(BF16)
            wr_lo = (wr - wr_hi.astype(F32)).astype(BF16)
            xp, xs = _moe_layer([xp, xs], wr_hi, wr_lo, moe_w_gate, moe_w_up, moe_w_down, j, fg, fb,
                                ne=ne, alpha=alpha)

    return (xp.reshape(bp, sp, d), xs.reshape(bs, ss, d),
            jnp.stack(ckv_p), jnp.stack(kr_p), jnp.stack(pool_p),
            jnp.stack(ckv_s), jnp.stack(kr_s), jnp.stack(pool_s))
```
